```python
import math
import jax, jax.numpy as jnp
from jax import lax
import numpy as np

D_MODEL = 2048
BATCH = 2
SEQ = 4096
DEPTH = 4
DEC_BATCH = 128
DEC_SEQ = 8
PAST_LEN = 8192
PAGE_SIZE = 128

DIFF_HEADS = 4
DIFF_KV_HEADS = 1
DIFF_D = 64
MLSTM_HEADS = 4
MLSTM_DQK = 128
MLSTM_DV = 256
MLSTM_CHUNK = 64
MLA_HEADS = 4
MLA_Q_RANK = 384
MLA_KV_RANK = 128
MLA_NOPE = 128
MLA_ROPE = 32
MLA_DV = 128
N_EXPERTS = 32
TOP_K = 4
D_EXPERT = 1024
SWIGLU_LIMIT = 7.0
SWIGLU_ALPHA = 1.702
MOE_BLOCK = 128
ROPE_THETA = 10000.0
Q_BLOCK = 128
EPS = 1e-6
N_MOD = 6

DIFF_W = DIFF_HEADS * 2 * DIFF_D
MLSTM_W = MLSTM_HEADS * MLSTM_DV
MLA_W = MLA_HEADS * MLA_DV
MIX_W = DIFF_W + MLSTM_W + MLA_W
N_IN = (DIFF_HEADS * 2 * DIFF_D + 2 * DIFF_KV_HEADS * 2 * DIFF_D
        + 2 * MLSTM_HEADS * MLSTM_DQK + 2 * MLSTM_HEADS * MLSTM_DV + 2 * MLSTM_HEADS
        + MLA_Q_RANK + MLA_KV_RANK + MLA_ROPE)
DIFF_SCALE = DIFF_D ** -0.5
MLA_SCALE = (MLA_NOPE + MLA_ROPE) ** -0.5

kernel_name = 'hybrid_diffattn_mlstm_mla_moe_step'


def _in_offsets():
    sizes = [DIFF_HEADS * 2 * DIFF_D, DIFF_KV_HEADS * 2 * DIFF_D, DIFF_KV_HEADS * 2 * DIFF_D,
             MLSTM_HEADS * MLSTM_DQK, MLSTM_HEADS * MLSTM_DQK, MLSTM_HEADS * MLSTM_DV,
             MLSTM_HEADS * MLSTM_DV, MLSTM_HEADS, MLSTM_HEADS,
             MLA_Q_RANK, MLA_KV_RANK, MLA_ROPE]
    return [int(o) for o in np.cumsum(sizes)[:-1]]


def rms_norm(x, g):
    xf = x.astype(jnp.float32)
    y = xf * lax.rsqrt(jnp.mean(xf * xf, axis=-1, keepdims=True) + EPS)
    return (y * g.astype(jnp.float32)).astype(x.dtype)


def rope(x, pos):
    half = x.shape[-1] // 2
    inv_freq = ROPE_THETA ** (-jnp.arange(half, dtype=jnp.float32) / half)
    ang = pos.astype(jnp.float32)[:, None] * inv_freq[None, :]
    shp = (pos.shape[0],) + (1,) * (x.ndim - 3) + (half,)
    cos = jnp.cos(ang).reshape(shp)
    sin = jnp.sin(ang).reshape(shp)
    xf = x.astype(jnp.float32)
    x1, x2 = xf[..., :half], xf[..., half:]
    return jnp.concatenate([x1 * cos - x2 * sin, x2 * cos + x1 * sin], axis=-1).astype(x.dtype)


def sdpa(q, k, v, q_pos, k_pos, scale):
    B, Tq, H, dk = q.shape
    G = k.shape[2]
    R = H // G
    s = jnp.einsum('bqgrd,bkgd->bgrqk', q.reshape(B, Tq, G, R, dk), k,
                   preferred_element_type=jnp.float32) * scale
    s = jnp.where(k_pos[None, :] <= q_pos[:, None], s, -jnp.inf)
    p = jax.nn.softmax(s, axis=-1).astype(v.dtype)
    o = jnp.einsum('bgrqk,bkgd->bqgrd', p, v)
    return o.reshape(B, Tq, H, v.shape[-1])


def blocked_sdpa(q, k, v, q_pos, k_pos, scale):
    B, Tq, H, dk = q.shape
    blk = math.gcd(Tq, Q_BLOCK)
    nb = Tq // blk
    qb = jnp.moveaxis(q.reshape(B, nb, blk, H, dk), 1, 0)
    pb = q_pos.reshape(nb, blk)
    ob = lax.map(lambda a: sdpa(a[0], k, v, a[1], k_pos, scale), (qb, pb))
    return jnp.moveaxis(ob, 0, 1).reshape(B, Tq, H, v.shape[-1])


def gather_pages(cache_l, page_table):
    g = cache_l[page_table]
    return g.reshape((g.shape[0], g.shape[1] * g.shape[2]) + g.shape[3:])


def diff_attention(dq, dk, dv, pos, past, p):
    B, T, _ = dq.shape
    q = rope(rms_norm(dq.reshape(B, T, DIFF_HEADS, 2, DIFF_D), p['g_dq']), pos)
    k = rope(rms_norm(dk.reshape(B, T, DIFF_KV_HEADS, 2, DIFF_D), p['g_dk']), pos)
    k_new = k.reshape(B, T, DIFF_KV_HEADS, 2 * DIFF_D)
    v_new = dv.reshape(B, T, DIFF_KV_HEADS, 2 * DIFF_D)
    if past is None:
        k_all, v_all, k_pos = k_new, v_new, pos
    else:
        pk, pv = past
        k_all = jnp.concatenate([pk, k_new], axis=1)
        v_all = jnp.concatenate([pv, v_new], axis=1)
        k_pos = jnp.concatenate([jnp.arange(pk.shape[1], dtype=pos.dtype), pos])
    lv = p['lam_diff'].astype(jnp.float32)
    lam_init = p['lam_init']
    lam = jnp.exp(jnp.sum(lv[0] * lv[1])) - jnp.exp(jnp.sum(lv[2] * lv[3])) + lam_init
    o1 = blocked_sdpa(q[..., 0, :], k_all[..., :DIFF_D], v_all, pos, k_pos, DIFF_SCALE)
    o2 = blocked_sdpa(q[..., 1, :], k_all[..., DIFF_D:], v_all, pos, k_pos, DIFF_SCALE)
    o = o1.astype(jnp.float32) - lam * o2.astype(jnp.float32)
    o = rms_norm(o, p['g_dsub']) * (1.0 - lam_init)
    return o.reshape(B, T, DIFF_W).astype(dq.dtype), k_new, v_new


def mlstm(mq, mk, mv, mo, mi, mf, state, p):
    f32 = jnp.float32
    B, T, _ = mq.shape
    H = MLSTM_HEADS
    q = mq.reshape(B, T, H, MLSTM_DQK).astype(f32)
    k = mk.reshape(B, T, H, MLSTM_DQK).astype(f32) * (MLSTM_DQK ** -0.5)
    v = mv.reshape(B, T, H, MLSTM_DV).astype(f32)
    ig = (mi + p['b_mi']).astype(f32)
    lf = jax.nn.log_sigmoid((mf + p['b_mf']).astype(f32))
    L = math.gcd(T, MLSTM_CHUNK)
    nc = T // L

    def chunks(a):
        return jnp.moveaxis(a.reshape((B, nc, L) + a.shape[2:]), 1, 0)

    causal = jnp.tril(jnp.ones((L, L), dtype=bool))[None, :, :, None]

    def step(carry, inp):
        C, n, m = carry
        qc, kc, vc, ic, fc = inp
        b = jnp.cumsum(fc, axis=1)
        logw = b[:, :, None, :] - b[:, None, :, :] + ic[:, None, :, :]
        logw = jnp.where(causal, logw, -jnp.inf)
        inter = b + m[:, None, :]
        m_t = jnp.maximum(inter, jnp.max(logw, axis=2))
        s = jnp.exp(logw - m_t[:, :, None, :]) * jnp.einsum('bthd,bshd->btsh', qc, kc)
        a = jnp.exp(inter - m_t)
        num = jnp.einsum('btsh,bshv->bthv', s, vc) + a[..., None] * jnp.einsum('bhvd,bthd->bthv', C, qc)
        den = jnp.sum(s, axis=2) + a * jnp.einsum('bhd,bthd->bth', n, qc)
        h = num / jnp.maximum(jnp.abs(den), jnp.exp(-m_t))[..., None]
        m_new = m_t[:, -1]
        w_end = jnp.exp(b[:, -1:, :] - b + ic - m_new[:, None, :])
        decay = jnp.exp(b[:, -1] + m - m_new)
        C_new = decay[..., None, None] * C + jnp.einsum('bsh,bshv,bshd->bhvd', w_end, vc, kc)
        n_new = decay[..., None] * n + jnp.einsum('bsh,bshd->bhd', w_end, kc)
        return (C_new, n_new, m_new), h

    init = (state[0].astype(f32), state[1].astype(f32), state[2].astype(f32))
    (C, n, m), h = lax.scan(step, init, (chunks(q), chunks(k), chunks(v), chunks(ig), chunks(lf)))
    h = jnp.moveaxis(h, 0, 1).reshape(B, T, H, MLSTM_DV)
    h = rms_norm(h, p['g_mout'].reshape(H, MLSTM_DV))
    y = jax.nn.sigmoid(mo.astype(f32)).reshape(B, T, H, MLSTM_DV) * h
    return y.reshape(B, T, MLSTM_W).astype(mq.dtype), (C, n, m)


def mla_core(q_nope, q_rope, ckv_all, kr_all, q_pos, k_pos, p):
    B, Tk, _ = ckv_all.shape
    kv = jnp.einsum('bsc,ce->bse', ckv_all, p['w_akvb']).reshape(B, Tk, MLA_HEADS, MLA_NOPE + MLA_DV)
    k_nope = rms_norm(kv[..., :MLA_NOPE], p['g_akn'])
    v = kv[..., MLA_NOPE:]
    k = jnp.concatenate([k_nope, jnp.broadcast_to(kr_all[:, :, None, :], (B, Tk, MLA_HEADS, MLA_ROPE))], axis=-1)
    q = jnp.concatenate([q_nope, q_rope], axis=-1)
    return blocked_sdpa(q, k, v, q_pos, k_pos, MLA_SCALE)


def mla(aq, akv, ar, pos, past, p):
    B, T, _ = aq.shape
    q = jnp.einsum('btr,re->bte', rms_norm(aq, p['g_aqa']), p['w_aqb']).reshape(B, T, MLA_HEADS, MLA_NOPE + MLA_ROPE)
    q_nope = rms_norm(q[..., :MLA_NOPE], p['g_aqn'])
    q_rope = rope(rms_norm(q[..., MLA_NOPE:], p['g_aqr']), pos)
    ckv = rms_norm(akv, p['g_akva'])
    kr = rope(rms_norm(ar, p['g_akr']), pos)
    if past is None:
        o = mla_core(q_nope, q_rope, ckv, kr, pos, pos, p)
    else:
        pc, pr = past
        c_all = jnp.concatenate([pc, ckv], axis=1)
        r_all = jnp.concatenate([pr, kr], axis=1)
        k_pos = jnp.concatenate([jnp.arange(pc.shape[1], dtype=pos.dtype), pos])
        o = lax.map(lambda a: mla_core(a[0][None], a[1][None], a[2][None], a[3][None], pos, k_pos, p)[0],
                    (q_nope, q_rope, c_all, r_all))
    return o.reshape(B, T, MLA_W), ckv, kr


def moe(h, p):
    T, D = h.shape
    logits = jnp.einsum('td,de->te', h, p['w_router'], preferred_element_type=jnp.float32) + p['b_router'].astype(jnp.float32)
    top_v, top_e = lax.top_k(logits, TOP_K)
    gate = jax.nn.softmax(top_v, axis=-1)
    flat_e = top_e.reshape(-1)
    flat_t = jnp.repeat(jnp.arange(T, dtype=jnp.int32), TOP_K)
    flat_g = gate.reshape(-1)
    order = jnp.argsort(flat_e)
    se = flat_e[order]
    counts = jnp.bincount(flat_e, length=N_EXPERTS)
    pcounts = (counts + MOE_BLOCK - 1) // MOE_BLOCK * MOE_BLOCK
    starts = jnp.cumsum(counts) - counts
    pends = jnp.cumsum(pcounts)
    pstarts = pends - pcounts
    dest = pstarts[se] + jnp.arange(T * TOP_K) - starts[se]
    n_rows = (-(-(T * TOP_K) // MOE_BLOCK) + N_EXPERTS) * MOE_BLOCK
    n_blocks = n_rows // MOE_BLOCK
    row_tok = jnp.full((n_rows,), T, dtype=jnp.int32).at[dest].set(flat_t[order])
    row_gate = jnp.zeros((n_rows,), h.dtype).at[dest].set(flat_g[order].astype(h.dtype))
    blk_e = jnp.minimum(jnp.searchsorted(pends, jnp.arange(n_blocks) * MOE_BLOCK, side='right'), N_EXPERTS - 1)
    h_pad = jnp.concatenate([h, jnp.zeros((1, D), h.dtype)], axis=0)
    xb = h_pad[row_tok].reshape(n_blocks, MOE_BLOCK, D)

    def expert_block(args):
        xe, e = args
        gu = xe @ p['w_gu'][e] + p['b_gu'][e]
        g_ = jnp.minimum(gu[:, :D_EXPERT], SWIGLU_LIMIT)
        u_ = jnp.clip(gu[:, D_EXPERT:], -SWIGLU_LIMIT, SWIGLU_LIMIT)
        act = g_ * jax.nn.sigmoid(SWIGLU_ALPHA * g_) * (u_ + 1.0)
        return act @ p['w_down'][e] + p['b_down'][e]

    yb = lax.map(expert_block, (xb, blk_e))
    y = jax.ops.segment_sum(yb.reshape(n_rows, D) * row_gate[:, None], row_tok, num_segments=T + 1)
    return y[:T]


def trunk_layer(x, c, pos, p, past, mstate):
    B, T, D = x.shape
    mod = (jnp.einsum('bd,de->be', jax.nn.silu(c), p['w_ada']) + p['b_ada']).reshape(B, N_MOD, D)
    sh1, sc1, g1, sh2, sc2, g2 = (mod[:, i, None, :] for i in range(N_MOD))
    h = rms_norm(x, p['g_norm1']) * (1.0 + sc1) + sh1
    proj = jnp.einsum('btd,de->bte', h, p['w_in'])
    dq, dk, dv, mq, mk, mv, mo, mi, mf, aq, akv, ar = jnp.split(proj, _in_offsets(), axis=-1)
    diff_past = None if past is None else past[:2]
    mla_past = None if past is None else past[2:]
    y_d, k_new, v_new = diff_attention(dq, dk, dv, pos, diff_past, p)
    y_m, mst = mlstm(mq, mk, mv, mo, mi, mf, mstate, p)
    y_a, ckv, kr = mla(aq, akv, ar, pos, mla_past, p)
    mix = jnp.concatenate([y_d, y_m, y_a], axis=-1)
    x = x + g1 * jnp.einsum('bte,ed->btd', mix, p['w_out'])
    h2 = rms_norm(x, p['g_norm2']) * (1.0 + sc2) + sh2
    x = x + g2 * moe(h2.reshape(B * T, D), p).reshape(B, T, D)
    return x, (k_new, v_new, ckv, kr, mst[0], mst[1], mst[2])


def setup_inputs(seed: int = 0) -> dict:
    key = jax.random.key(seed)
    ks = jax.random.split(key, 48)
    f32 = jnp.float32

    def nrm(i, shape, scale):
        return scale * jax.random.normal(ks[i], shape, f32)

    def gain(i, shape):
        return 1.0 + 0.02 * jax.random.normal(ks[i], shape, f32)

    n_pages = PAST_LEN // PAGE_SIZE
    n_pool = (5 * DEC_BATCH * n_pages + 3) // 4
    page_table = jax.random.permutation(ks[0], n_pool)[: DEC_BATCH * n_pages].reshape(DEC_BATCH, n_pages).astype(jnp.int32)
    L, D, E, F = DEPTH, D_MODEL, N_EXPERTS, D_EXPERT
    return {
        'x_prompt': nrm(1, (BATCH, SEQ, D), 1.0),
        'x_sample': nrm(2, (DEC_BATCH, DEC_SEQ, D), 1.0),
        'c_prompt': nrm(3, (BATCH, D), 1.0),
        'c_sample': nrm(4, (DEC_BATCH, D), 1.0),
        'cache_diff_k': nrm(5, (L, n_pool, PAGE_SIZE, DIFF_KV_HEADS, 2 * DIFF_D), 1.0),
        'cache_diff_v': nrm(6, (L, n_pool, PAGE_SIZE, DIFF_KV_HEADS, 2 * DIFF_D), 1.0),
        'cache_mla_ckv': nrm(7, (L, n_pool, PAGE_SIZE, MLA_KV_RANK), 1.0),
        'cache_mla_kr': nrm(8, (L, n_pool, PAGE_SIZE, MLA_ROPE), 1.0),
        'state_mlstm_c': nrm(9, (L, DEC_BATCH, MLSTM_HEADS, MLSTM_DV, MLSTM_DQK), 0.5),
        'state_mlstm_n': nrm(10, (L, DEC_BATCH, MLSTM_HEADS, MLSTM_DQK), 0.5),
        'state_mlstm_m': nrm(11, (L, DEC_BATCH, MLSTM_HEADS), 1.0),
        'page_table': page_table,
        'w_ada': nrm(12, (L, D, N_MOD * D), 0.5 * D ** -0.5),
        'b_ada': nrm(13, (L, N_MOD * D), 0.01),
        'g_norm1': gain(14, (L, D)),
        'g_norm2': gain(15, (L, D)),
        'w_in': nrm(16, (L, D, N_IN), D ** -0.5),
        'g_dq': gain(17, (L, 2, DIFF_D)),
        'g_dk': gain(18, (L, 2, DIFF_D)),
        'lam_diff': nrm(19, (L, 4, DIFF_D), 0.1),
        'g_dsub': gain(20, (L, 2 * DIFF_D)),
        'b_mi': nrm(21, (L, MLSTM_HEADS), 0.1),
        'b_mf': 3.0 + nrm(22, (L, MLSTM_HEADS), 0.1),
        'g_mout': gain(23, (L, MLSTM_W)),
        'g_aqa': gain(24, (L, MLA_Q_RANK)),
        'w_aqb': nrm(25, (L, MLA_Q_RANK, MLA_HEADS * (MLA_NOPE + MLA_ROPE)), MLA_Q_RANK ** -0.5),
        'g_akva': gain(26, (L, MLA_KV_RANK)),
        'w_akvb': nrm(27, (L, MLA_KV_RANK, MLA_HEADS * (MLA_NOPE + MLA_DV)), MLA_KV_RANK ** -0.5),
        'g_aqn': gain(28, (L, MLA_NOPE)),
        'g_aqr': gain(29, (L, MLA_ROPE)),
        'g_akn': gain(30, (L, MLA_NOPE)),
        'g_akr': gain(31, (L, MLA_ROPE)),
        'w_out': nrm(32, (L, MIX_W, D), MIX_W ** -0.5),
        'w_router': nrm(33, (L, D, E), D ** -0.5),
        'b_router': nrm(34, (L, E), 0.01),
        'w_gu': nrm(35, (L, E, D, 2 * F), D ** -0.5),
        'b_gu': nrm(36, (L, E, 2 * F), 0.01),
        'w_down': nrm(37, (L, E, F, D), F ** -0.5),
        'b_down': nrm(38, (L, E, D), 0.01),
    }


def reference(x_prompt, x_sample, c_prompt, c_sample, cache_diff_k, cache_diff_v, cache_mla_ckv, cache_mla_kr,
              state_mlstm_c, state_mlstm_n, state_mlstm_m, page_table, w_ada, b_ada, g_norm1, g_norm2, w_in,
              g_dq, g_dk, lam_diff, g_dsub, b_mi, b_mf, g_mout, g_aqa, w_aqb, g_akva, w_akvb, g_aqn, g_aqr,
              g_akn, g_akr, w_out, w_router, b_router, w_gu, b_gu, w_down, b_down):
    f32 = jnp.float32
    bp, tp = x_prompt.shape[0], x_prompt.shape[1]
    pos_p = jnp.arange(tp, dtype=jnp.int32)
    pos_s = PAST_LEN + jnp.arange(x_sample.shape[1], dtype=jnp.int32)
    zero_state = (jnp.zeros((bp, MLSTM_HEADS, MLSTM_DV, MLSTM_DQK), f32),
                  jnp.zeros((bp, MLSTM_HEADS, MLSTM_DQK), f32),
                  jnp.zeros((bp, MLSTM_HEADS), f32))
    hp, hs = x_prompt, x_sample
    new_p = [[] for _ in range(7)]
    new_s = [[] for _ in range(7)]
    for l in range(DEPTH):
        p = {
            'w_ada': w_ada[l], 'b_ada': b_ada[l], 'g_norm1': g_norm1[l], 'g_norm2': g_norm2[l],
            'w_in': w_in[l], 'g_dq': g_dq[l], 'g_dk': g_dk[l], 'lam_diff': lam_diff[l], 'g_dsub': g_dsub[l],
            'lam_init': 0.8 - 0.6 * math.exp(-0.3 * l),
            'b_mi': b_mi[l], 'b_mf': b_mf[l], 'g_mout': g_mout[l],
            'g_aqa': g_aqa[l], 'w_aqb': w_aqb[l], 'g_akva': g_akva[l], 'w_akvb': w_akvb[l],
            'g_aqn': g_aqn[l], 'g_aqr': g_aqr[l], 'g_akn': g_akn[l], 'g_akr': g_akr[l],
            'w_out': w_out[l], 'w_router': w_router[l], 'b_router': b_router[l],
            'w_gu': w_gu[l], 'b_gu': b_gu[l], 'w_down': w_down[l], 'b_down': b_down[l],
        }
        hp, st_p = trunk_layer(hp, c_prompt, pos_p, p, None, zero_state)
        past = (gather_pages(cache_diff_k[l], page_table), gather_pages(cache_diff_v[l], page_table),
                gather_pages(cache_mla_ckv[l], page_table), gather_pages(cache_mla_kr[l], page_table))
        hs, st_s = trunk_layer(hs, c_sample, pos_s, p, past,
                               (state_mlstm_c[l], state_mlstm_n[l], state_mlstm_m[l]))
        for lst, a in zip(new_p, st_p):
            lst.append(a.astype(x_prompt.dtype))
        for lst, a in zip(new_s, st_s):
            lst.append(a.astype(x_sample.dtype))
    dk_p, dv_p, ckv_p, kr_p, mc_p, mn_p, mm_p = [jnp.stack(a) for a in new_p]
    dk_s, dv_s, ckv_s, kr_s, mc_s, mn_s, mm_s = [jnp.stack(a) for a in new_s]
    return (hp, hs, dk_p, dv_p, ckv_p, kr_p, mc_p, mn_p, mm_p, dk_s, dv_s, ckv_s, kr_s, mc_s, mn_s, mm_s)
```

```python
import functools
import math

import jax
import jax.numpy as jnp
from jax import lax
from jax.experimental import pallas as pl
from jax.experimental.pallas import tpu as pltpu

F32 = jnp.float32
BF16 = jnp.bfloat16

PAST_LEN = 8192
DIFF_HEADS = 4
DIFF_D = 64
MLSTM_HEADS = 4
MLSTM_DQK = 128
MLSTM_DV = 256
MLA_HEADS = 4
MLA_Q_RANK = 384
MLA_KV_RANK = 128
MLA_NOPE = 128
MLA_ROPE = 32
MLA_DV = 128
N_EXPERTS = 32
TOP_K = 4
SWIGLU_LIMIT = 7.0
SWIGLU_ALPHA = 1.702
ROPE_THETA = 10000.0
EPS = 1e-6
N_MOD = 6
DIFF_SCALE = DIFF_D ** -0.5
MLA_SCALE = (MLA_NOPE + MLA_ROPE) ** -0.5

LANES = 128
SUBLANES = 8
VMEM_LIMIT_BYTES = 56 * 1024 * 1024
NEG_BIG = -1e30

_NT = (((1,), (1,)), ((), ()))
_TN = (((0,), (0,)), ((), ()))


def _params(semantics, vmem=VMEM_LIMIT_BYTES):
    return pltpu.CompilerParams(dimension_semantics=semantics, vmem_limit_bytes=vmem)


def _ada_kernel(c_ref, w_ref, b_ref, o_ref):
    c = c_ref[...]
    a = (c * jax.nn.sigmoid(c)).astype(BF16)
    o_ref[...] = jnp.dot(a, w_ref[...].astype(BF16), preferred_element_type=F32) + b_ref[...]


def ada_mod(c, w_ada, b_ada, tn=1024):
    L, D, N = w_ada.shape
    Bc = c.shape[0]
    tn = math.gcd(N, tn)
    return pl.pallas_call(
        _ada_kernel,
        grid=(L, N // tn),
        in_specs=[pl.BlockSpec((Bc, D), lambda l, j: (0, 0)),
                  pl.BlockSpec((None, D, tn), lambda l, j: (l, 0, j)),
                  pl.BlockSpec((None, 1, tn), lambda l, j: (l, 0, j))],
        out_specs=pl.BlockSpec((None, Bc, tn), lambda l, j: (l, 0, j)),
        out_shape=jax.ShapeDtypeStruct((L, Bc, N), F32),
        compiler_params=_params(("arbitrary", "arbitrary")),
        name="ada_mod",
    )(c, w_ada, b_ada.reshape(L, 1, N))


def _matmul_kernel(x_ref, w_ref, o_ref):
    o_ref[...] = jnp.dot(x_ref[...].astype(BF16), w_ref[...].astype(BF16),
                         preferred_element_type=F32).astype(o_ref.dtype)


def matmul(x, w, tm, out_dtype=F32):
    M, K = x.shape
    N = w.shape[1]
    tm = min(tm, M)
    return pl.pallas_call(
        _matmul_kernel,
        grid=(M // tm,),
        in_specs=[pl.BlockSpec((tm, K), lambda i: (i, 0)),
                  pl.BlockSpec((K, N), lambda i: (0, 0))],
        out_specs=pl.BlockSpec((tm, N), lambda i: (i, 0)),
        out_shape=jax.ShapeDtypeStruct((M, N), out_dtype),
        compiler_params=_params(("arbitrary",)),
        name="matmul",
    )(x, w)


def _rms(x):
    return x * lax.rsqrt(jnp.mean(x * x, axis=-1, keepdims=True) + EPS)


def _inproj_kernel(x_ref, g_ref, sc_ref, sh_ref, w_ref, o_ref, h_scr):
    @pl.when(pl.program_id(1) == 0)
    def _():
        y = _rms(x_ref[...]) * g_ref[...]
        h_scr[...] = (y * (1.0 + sc_ref[...]) + sh_ref[...]).astype(BF16)

    o_ref[...] = jnp.dot(h_scr[...], w_ref[...], preferred_element_type=F32)


def _mod_spec(mod, tm, rows_per_seq):
    if mod.ndim == 3:
        tiles = rows_per_seq // tm
        return pl.BlockSpec((None, 1, mod.shape[-1]), lambda i, *_: (i // tiles, 0, 0))
    return pl.BlockSpec((tm, mod.shape[-1]), lambda i, *_: (i, 0))


def in_proj(x, g, sc, sh, w, tm, tn, rows_per_seq):
    T, D = x.shape
    N = w.shape[1]
    tm = min(tm, T)
    return pl.pallas_call(
        _inproj_kernel,
        grid=(T // tm, N // tn),
        in_specs=[pl.BlockSpec((tm, D), lambda i, j: (i, 0)),
                  pl.BlockSpec((1, D), lambda i, j: (0, 0)),
                  _mod_spec(sc, tm, rows_per_seq),
                  _mod_spec(sh, tm, rows_per_seq),
                  pl.BlockSpec((D, tn), lambda i, j: (0, j))],
        out_specs=pl.BlockSpec((tm, tn), lambda i, j: (i, j)),
        out_shape=jax.ShapeDtypeStruct((T, N), F32),
        scratch_shapes=[pltpu.VMEM((tm, D), BF16)],
        compiler_params=_params(("arbitrary", "arbitrary")),
        name="in_proj",
    )(x, g.reshape(1, D), sc, sh, w)


def _outproj_kernel(mix_ref, x_ref, g1_ref, w_ref, gn_ref, sc_ref, sh_ref, wr_ref, br_ref,
                    xo_ref, h2_ref, lg_ref):
    x = x_ref[...] + g1_ref[...] * jnp.dot(mix_ref[...], w_ref[...], preferred_element_type=F32)
    xo_ref[...] = x
    h2 = (_rms(x) * gn_ref[...]) * (1.0 + sc_ref[...]) + sh_ref[...]
    h2_ref[...] = h2.astype(BF16)
    lg_ref[...] = jnp.dot(h2, wr_ref[...], precision=lax.Precision.HIGHEST,
                          preferred_element_type=F32) + br_ref[...]


def out_proj(mix, x, g1, w, gn, sc, sh, wr, br, tm, rows_per_seq):
    T, D = x.shape
    K = mix.shape[1]
    E = wr.shape[1]
    tm = min(tm, T)
    return pl.pallas_call(
        _outproj_kernel,
        grid=(T // tm,),
        in_specs=[pl.BlockSpec((tm, K), lambda i: (i, 0)),
                  pl.BlockSpec((tm, D), lambda i: (i, 0)),
                  _mod_spec(g1, tm, rows_per_seq),
                  pl.BlockSpec((K, D), lambda i: (0, 0)),
                  pl.BlockSpec((1, D), lambda i: (0, 0)),
                  _mod_spec(sc, tm, rows_per_seq),
                  _mod_spec(sh, tm, rows_per_seq),
                  pl.BlockSpec((D, E), lambda i: (0, 0)),
                  pl.BlockSpec((1, E), lambda i: (0, 0))],
        out_specs=[pl.BlockSpec((tm, D), lambda i: (i, 0)),
                   pl.BlockSpec((tm, D), lambda i: (i, 0)),
                   pl.BlockSpec((tm, E), lambda i: (i, 0))],
        out_shape=[jax.ShapeDtypeStruct((T, D), F32),
                   jax.ShapeDtypeStruct((T, D), BF16),
                   jax.ShapeDtypeStruct((T, E), F32)],
        compiler_params=_params(("arbitrary",)),
        name="out_proj",
    )(mix, x, g1, w, gn.reshape(1, D), sc, sh, wr, br.reshape(1, E))


def _flash_kernel(*refs, has_rope):
    if has_rope:
        q_ref, k_ref, v_ref, qr_ref, kr_ref, o_ref, m_scr, l_scr, acc_scr = refs
    else:
        q_ref, k_ref, v_ref, o_ref, m_scr, l_scr, acc_scr = refs
    i = pl.program_id(2)
    j = pl.program_id(3)

    @pl.when(j == 0)
    def _():
        m_scr[...] = jnp.full(m_scr.shape, NEG_BIG, F32)
        l_scr[...] = jnp.zeros(l_scr.shape, F32)
        acc_scr[...] = jnp.zeros(acc_scr.shape, F32)

    def step(masked):
        s = lax.dot_general(q_ref[...], k_ref[...], _NT, preferred_element_type=F32)
        if has_rope:
            s = s + lax.dot_general(qr_ref[...], kr_ref[...], _NT, preferred_element_type=F32)
        if masked:
            row = lax.broadcasted_iota(jnp.int32, s.shape, 0)
            col = lax.broadcasted_iota(jnp.int32, s.shape, 1)
            s = jnp.where(col <= row, s, NEG_BIG)
        m_prev = m_scr[...]
        m_new = jnp.maximum(m_prev, jnp.max(s, axis=1, keepdims=True))
        alpha = jnp.exp(m_prev - m_new)
        p = jnp.exp(s - m_new)
        l_scr[...] = alpha * l_scr[...] + jnp.sum(p, axis=1, keepdims=True)
        acc_scr[...] = alpha * acc_scr[...] + jnp.dot(p.astype(BF16), v_ref[...],
                                                      preferred_element_type=F32)
        m_scr[...] = m_new

    @pl.when(j < i)
    def _():
        step(False)

    @pl.when(j == i)
    def _():
        step(True)
        o_ref[...] = acc_scr[...] / l_scr[...]


def flash_attention(q, k, v, kmap, vmap, t, qr=None, kr=None):
    B, H, T, dk = q.shape
    dv = v.shape[-1]
    nb = T // t
    in_specs = [pl.BlockSpec((None, None, t, dk), lambda b, h, i, j: (b, h, i, 0)),
                pl.BlockSpec((None, None, t, dk), lambda b, h, i, j: (b, kmap(h), jnp.minimum(i, j), 0)),
                pl.BlockSpec((None, None, t, dv), lambda b, h, i, j: (b, vmap(h), jnp.minimum(i, j), 0))]
    args = [q, k, v]
    if qr is not None:
        dr = qr.shape[-1]
        in_specs += [pl.BlockSpec((None, None, t, dr), lambda b, h, i, j: (b, h, i, 0)),
                     pl.BlockSpec((None, None, t, dr), lambda b, h, i, j: (b, 0, jnp.minimum(i, j), 0))]
        args += [qr, kr]
    return pl.pallas_call(
        functools.partial(_flash_kernel, has_rope=qr is not None),
        grid=(B, H, nb, nb),
        in_specs=in_specs,
        out_specs=pl.BlockSpec((None, None, t, dv), lambda b, h, i, j: (b, h, i, 0)),
        out_shape=jax.ShapeDtypeStruct((B, H, T, dv), F32),
        scratch_shapes=[pltpu.VMEM((t, 1), F32), pltpu.VMEM((t, 1), F32), pltpu.VMEM((t, dv), F32)],
        compiler_params=_params(("arbitrary",) * 4),
        name="flash_attention",
    )(*args)


N_PAGED = 4


def _sample_attn_kernel(pt_ref, lam_ref, qd_ref, knew_ref, vnew_ref, qn_ref, qr_ref, wkt_ref, wv_ref,
                        cnew_ref, rnew_ref, kc_hbm, vc_hbm, cc_hbm, rc_hbm, od_ref, oa_ref,
                        kbuf, vbuf, cbuf, rbuf, sem, s_scr, *, layer, n_pages, page, chunk, ts):
    b = pl.program_id(0)
    nb = pl.num_programs(0)
    slot = b % 2
    tp = n_pages * page
    n_chunks = tp // chunk
    hd = MLA_HEADS
    rd = 2 * DIFF_HEADS * ts
    ra = MLA_HEADS * ts
    hbm = (kc_hbm, vc_hbm, cc_hbm, rc_hbm)
    bufs = (kbuf, vbuf, cbuf, rbuf)

    def page_copies(seq, sl, p):
        pg = pt_ref[seq * n_pages + p]
        rows = pl.ds(pl.multiple_of(p * page, page), page)
        return [pltpu.make_async_copy(hbm[a].at[layer, pg], bufs[a].at[sl, rows], sem.at[a, sl])
                for a in range(N_PAGED)]

    def start_gather(seq, sl):
        def body(p, carry):
            for cp in page_copies(seq, sl, p):
                cp.start()
            return carry
        lax.fori_loop(0, n_pages, body, 0)

    def wait_gather(seq, sl):
        def body(p, carry):
            for cp in page_copies(seq, sl, p):
                cp.wait()
            return carry
        lax.fori_loop(0, n_pages, body, 0)

    @pl.when(b == 0)
    def _():
        for buf in bufs:
            for sl in range(2):
                buf[sl, tp:tp + page, :] = jnp.zeros((page, buf.shape[-1]), F32)
        start_gather(0, 0)

    @pl.when(b + 1 < nb)
    def _():
        start_gather(b + 1, 1 - slot)

    kbuf[slot, tp:tp + ts, :] = knew_ref[...]
    vbuf[slot, tp:tp + ts, :] = vnew_ref[...]
    cbuf[slot, tp:tp + ts, :] = cnew_ref[...]
    rbuf[slot, tp:tp + ts, :] = rnew_ref[...]

    qd = qd_ref[...]
    qr = qr_ref[...]
    wkt = wkt_ref[...]
    qabs = jnp.dot(qn_ref[...], wkt, preferred_element_type=F32).astype(BF16)
    lhs = jnp.concatenate([wkt, qabs], axis=0)

    def scores(off, n):
        kc = kbuf[slot, pl.ds(off, n), :].astype(BF16)
        cc = cbuf[slot, pl.ds(off, n), :].astype(BF16)
        rc = rbuf[slot, pl.ds(off, n), :].astype(BF16)
        sd = lax.dot_general(qd, kc, _NT, preferred_element_type=F32)
        big = lax.dot_general(lhs, cc, _NT, preferred_element_type=F32)
        sr = lax.dot_general(qr, rc, _NT, preferred_element_type=F32)
        parts = [sd]
        for h in range(hd):
            kn = big[h * MLA_NOPE:(h + 1) * MLA_NOPE, :]
            ss = jnp.sum(kn * kn, axis=0, keepdims=True)
            rinv = lax.rsqrt(ss * (1.0 / MLA_NOPE) + EPS)
            lo = hd * MLA_NOPE + h * ts
            parts.append(big[lo:lo + ts, :] * rinv + sr[h * ts:(h + 1) * ts, :])
        return jnp.concatenate(parts, axis=0)

    def weighted(p, off, n):
        pb = p.astype(BF16)
        vc = vbuf[slot, pl.ds(off, n), :].astype(BF16)
        cc = cbuf[slot, pl.ds(off, n), :].astype(BF16)
        return (jnp.dot(pb[:rd], vc, preferred_element_type=F32),
                jnp.dot(pb[rd:], cc, preferred_element_type=F32))

    s_tail = scores(tp, page)
    row = lax.broadcasted_iota(jnp.int32, s_tail.shape, 0)
    col = lax.broadcasted_iota(jnp.int32, s_tail.shape, 1)
    s_tail = jnp.where(col <= row % ts, s_tail, NEG_BIG)
    m0 = jnp.max(s_tail, axis=1, keepdims=True)

    wait_gather(b, slot)

    def pass1(c, m):
        s = scores(pl.multiple_of(c * chunk, chunk), chunk)
        s_scr[c] = s
        return jnp.maximum(m, jnp.max(s, axis=1, keepdims=True))

    m = lax.fori_loop(0, n_chunks, pass1, m0)

    p_tail = jnp.exp(s_tail - m)
    l0 = jnp.sum(p_tail, axis=1, keepdims=True)
    accd0, acca0 = weighted(p_tail, tp, page)

    def pass2(c, carry):
        l, accd, acca = carry
        p = jnp.exp(s_scr[c] - m)
        dd, da = weighted(p, pl.multiple_of(c * chunk, chunk), chunk)
        return l + jnp.sum(p, axis=1, keepdims=True), accd + dd, acca + da

    l, accd, acca = lax.fori_loop(0, n_chunks, pass2, (l0, accd0, acca0))

    on = accd / l[:rd]
    half = rd // 2
    od_ref[...] = on[:half] - lam_ref[0, 0] * on[half:]
    lat = (acca / l[rd:]).astype(BF16)
    full = jnp.dot(lat, wv_ref[...], preferred_element_type=F32)
    oa_ref[...] = jnp.concatenate(
        [full[h * ts:(h + 1) * ts, h * MLA_DV:(h + 1) * MLA_DV] for h in range(hd)], axis=0)


def sample_attention(layer, page_table, lam, qd, knew, vnew, qn, qr, wkt, wv, cnew, rnew,
                     cache_k, cache_v, cache_c, cache_r, chunk=1024):
    DB, n_pages = page_table.shape
    page = cache_k.shape[2]
    ts = knew.shape[1]
    rd = 2 * DIFF_HEADS * ts
    ra = MLA_HEADS * ts
    tp = n_pages * page
    chunk = min(chunk, tp)
    kern = functools.partial(_sample_attn_kernel, layer=layer, n_pages=n_pages, page=page, chunk=chunk, ts=ts)
    seq3 = lambda b, pt: (b, 0, 0)
    const2 = lambda b, pt: (0, 0)
    any_spec = pl.BlockSpec(memory_space=pl.ANY)
    grid_spec = pltpu.PrefetchScalarGridSpec(
        num_scalar_prefetch=1,
        grid=(DB,),
        in_specs=[pl.BlockSpec(memory_space=pltpu.SMEM),
                  pl.BlockSpec((None, rd, 2 * DIFF_D), seq3),
                  pl.BlockSpec((None, ts, 2 * DIFF_D), seq3),
                  pl.BlockSpec((None, ts, 2 * DIFF_D), seq3),
                  pl.BlockSpec((None, ra, MLA_HEADS * MLA_NOPE), seq3),
                  pl.BlockSpec((None, ra, MLA_ROPE), seq3),
                  pl.BlockSpec((MLA_HEADS * MLA_NOPE, MLA_KV_RANK), const2),
                  pl.BlockSpec((MLA_KV_RANK, MLA_HEADS * MLA_DV), const2),
                  pl.BlockSpec((None, ts, MLA_KV_RANK), seq3),
                  pl.BlockSpec((None, ts, MLA_ROPE), seq3),
                  any_spec, any_spec, any_spec, any_spec],
        out_specs=[pl.BlockSpec((None, rd // 2, 2 * DIFF_D), seq3),
                   pl.BlockSpec((None, ra, MLA_DV), seq3)],
        scratch_shapes=[pltpu.VMEM((2, tp + page, 2 * DIFF_D), F32),
                        pltpu.VMEM((2, tp + page, 2 * DIFF_D), F32),
                        pltpu.VMEM((2, tp + page, MLA_KV_RANK), F32),
                        pltpu.VMEM((2, tp + page, MLA_ROPE), F32),
                        pltpu.SemaphoreType.DMA((N_PAGED, 2)),
                        pltpu.VMEM((tp // chunk, rd + ra, chunk), F32)],
    )
    return pl.pallas_call(
        kern,
        grid_spec=grid_spec,
        out_shape=[jax.ShapeDtypeStruct((DB, rd // 2, 2 * DIFF_D), F32),
                   jax.ShapeDtypeStruct((DB, ra, MLA_DV), F32)],
        compiler_params=_params(("arbitrary",)),
        name="sample_attention",
    )(page_table.reshape(-1), lam, qd, knew, vnew, qn, qr, wkt, wv, cnew, rnew,
      cache_k, cache_v, cache_c, cache_r)


def _log_sigmoid(x):
    return jnp.minimum(x, 0.0) - jnp.log(1.0 + jnp.exp(-jnp.abs(x)))


def _mlstm_kernel(q_ref, k_ref, v_ref, o_ref, icol_ref, irow_ref, fcol_ref, frow_ref, g_ref,
                  c0_ref, n0_ref, m0_ref, y_ref, c_ref, n_ref, m_ref, *, L):
    ci = pl.program_id(2)

    @pl.when(ci == 0)
    def _():
        c_ref[...] = c0_ref[...]
        n_ref[...] = n0_ref[...]
        m_ref[...] = m0_ref[...]

    q = q_ref[...]
    k = k_ref[...]
    v = v_ref[...]
    i_col = icol_ref[...]
    i_row = irow_ref[...]
    f_col = _log_sigmoid(fcol_ref[...])
    f_row = _log_sigmoid(frow_ref[...])
    C = c_ref[...]
    n = n_ref[...]
    m_prev = m_ref[...]

    t_idx = lax.broadcasted_iota(jnp.int32, (L, L), 0)
    s_idx = lax.broadcasted_iota(jnp.int32, (L, L), 1)
    causal = s_idx <= t_idx
    b_col = jnp.sum(jnp.where(causal, f_row, 0.0), axis=1, keepdims=True)
    b_row = jnp.sum(jnp.where(t_idx <= s_idx, f_col, 0.0), axis=0, keepdims=True)
    b_end = b_col[L - 1:L, :]

    logw = jnp.where(causal, b_col - b_row + i_row, -jnp.inf)
    inter = b_col + m_prev
    m_t = jnp.maximum(inter, jnp.max(logw, axis=1, keepdims=True))
    qk = lax.dot_general(q, k, _NT, preferred_element_type=F32) * (MLSTM_DQK ** -0.5)
    s = jnp.exp(logw - m_t) * qk
    a = jnp.exp(inter - m_t)
    qf = q.astype(F32)
    num = jnp.dot(s.astype(BF16), v, preferred_element_type=F32) \
        + a * lax.dot_general(q, C.astype(BF16), _NT, preferred_element_type=F32)
    den = jnp.sum(s, axis=1, keepdims=True) + a * jnp.sum(qf * n, axis=1, keepdims=True)
    h = num / jnp.maximum(jnp.abs(den), jnp.exp(-m_t))

    m_new = m_t[L - 1:L, :]
    w_end = jnp.exp(b_end - b_col + i_col - m_new)
    decay = jnp.exp(b_end + m_prev - m_new)
    k_scale = MLSTM_DQK ** -0.5
    wv = (w_end * v.astype(F32)).astype(BF16)
    c_ref[...] = decay * C + k_scale * lax.dot_general(wv, k, _TN, preferred_element_type=F32)
    n_ref[...] = decay * n + k_scale * jnp.sum(w_end * k.astype(F32), axis=0, keepdims=True)
    m_ref[...] = m_new

    hn = _rms(h) * g_ref[...]
    y_ref[...] = (jax.nn.sigmoid(o_ref[...]) * hn).astype(y_ref.dtype)


def mlstm(q, k, v, o, ig, fg, g, c0, n0, m0, L):
    B, T, _ = q.shape
    H, DQK, DV = MLSTM_HEADS, MLSTM_DQK, MLSTM_DV
    nc = T // L
    icol = jnp.transpose(ig, (0, 2, 1))[..., None]
    irow = jnp.transpose(ig, (0, 2, 1))[:, :, None, :]
    fcol = jnp.transpose(fg, (0, 2, 1))[..., None]
    frow = jnp.transpose(fg, (0, 2, 1))[:, :, None, :]
    tok = lambda b, h, c: (b, c, h)
    col = lambda b, h, c: (b, h, c, 0)
    rowm = lambda b, h, c: (b, h, 0, c)
    st = lambda b, h, c: (b, h, 0, 0)
    y, c, n, m = pl.pallas_call(
        functools.partial(_mlstm_kernel, L=L),
        grid=(B, H, nc),
        in_specs=[pl.BlockSpec((None, L, DQK), tok),
                  pl.BlockSpec((None, L, DQK), tok),
                  pl.BlockSpec((None, L, DV), tok),
                  pl.BlockSpec((None, L, DV), tok),
                  pl.BlockSpec((None, None, L, 1), col),
                  pl.BlockSpec((None, None, 1, L), rowm),
                  pl.BlockSpec((None, None, L, 1), col),
                  pl.BlockSpec((None, None, 1, L), rowm),
                  pl.BlockSpec((1, DV), lambda b, h, c: (0, h)),
                  pl.BlockSpec((None, None, DV, DQK), st),
                  pl.BlockSpec((None, None, 1, DQK), st),
                  pl.BlockSpec((None, None, 1, 1), st)],
        out_specs=[pl.BlockSpec((None, L, DV), tok),
                   pl.BlockSpec((None, None, DV, DQK), st),
                   pl.BlockSpec((None, None, 1, DQK), st),
                   pl.BlockSpec((None, None, 1, 1), st)],
        out_shape=[jax.ShapeDtypeStruct((B, T, H * DV), BF16),
                   jax.ShapeDtypeStruct((B, H, DV, DQK), F32),
                   jax.ShapeDtypeStruct((B, H, 1, DQK), F32),
                   jax.ShapeDtypeStruct((B, H, 1, 1), F32)],
        compiler_params=_params(("arbitrary",) * 3),
        name="mlstm",
    )(q, k, v, o, icol, irow, fcol, frow, g.reshape(1, H * DV),
      c0, n0.reshape(B, H, 1, DQK), m0.reshape(B, H, 1, 1))
    return y, c, n.reshape(B, H, DQK), m.reshape(B, H)


def _moe_kernel(be_ref, nu_ref, x_ref, gate_ref, wgu_ref, bgu_ref, wd_ref, bd_ref, o_ref, *, f):
    blk = pl.program_id(0)

    @pl.when(blk < nu_ref[0])
    def _():
        gu = jnp.dot(x_ref[...], wgu_ref[...], preferred_element_type=F32) + bgu_ref[...]
        g_ = jnp.minimum(gu[:, :f], SWIGLU_LIMIT)
        u_ = jnp.clip(gu[:, f:], -SWIGLU_LIMIT, SWIGLU_LIMIT)
        act = g_ * jax.nn.sigmoid(SWIGLU_ALPHA * g_) * (u_ + 1.0)
        y = jnp.dot(act.astype(BF16), wd_ref[...], preferred_element_type=F32) + bd_ref[...]
        o_ref[...] = y * gate_ref[...]

    @pl.when(blk >= nu_ref[0])
    def _():
        o_ref[...] = jnp.zeros(o_ref.shape, F32)


def moe_experts(blk_e, n_used, xb, row_gate, w_gu, b_gu, w_down, b_down, tm):
    R, D = xb.shape
    E, _, F2 = w_gu.shape
    f = F2 // 2
    grid_spec = pltpu.PrefetchScalarGridSpec(
        num_scalar_prefetch=2,
        grid=(R // tm,),
        in_specs=[pl.BlockSpec((tm, D), lambda i, be, nu: (i, 0)),
                  pl.BlockSpec((tm, 1), lambda i, be, nu: (i, 0)),
                  pl.BlockSpec((None, D, F2), lambda i, be, nu: (be[i], 0, 0)),
                  pl.BlockSpec((None, 1, F2), lambda i, be, nu: (be[i], 0, 0)),
                  pl.BlockSpec((None, f, D), lambda i, be, nu: (be[i], 0, 0)),
                  pl.BlockSpec((None, 1, D), lambda i, be, nu: (be[i], 0, 0))],
        out_specs=pl.BlockSpec((tm, D), lambda i, be, nu: (i, 0)),
    )
    return pl.pallas_call(
        functools.partial(_moe_kernel, f=f),
        grid_spec=grid_spec,
        out_shape=jax.ShapeDtypeStruct((R, D), F32),
        compiler_params=_params(("arbitrary",)),
        name="moe_experts",
    )(blk_e, n_used, xb, row_gate.reshape(R, 1), w_gu, b_gu.reshape(E, 1, F2), w_down,
      b_down.reshape(E, 1, D))


def moe(h2, logits, w_gu, b_gu, w_down, b_down, tm):
    T, D = h2.shape
    E = N_EXPERTS
    top_v, top_e = lax.top_k(logits, TOP_K)
    gate = jax.nn.softmax(top_v, axis=-1)
    flat_e = top_e.reshape(-1)
    flat_t = jnp.repeat(jnp.arange(T, dtype=jnp.int32), TOP_K)
    flat_g = gate.reshape(-1)
    order = jnp.argsort(flat_e)
    se = flat_e[order]
    counts = jnp.bincount(flat_e, length=E)
    pcounts = (counts + tm - 1) // tm * tm
    starts = jnp.cumsum(counts) - counts
    pends = jnp.cumsum(pcounts)
    pstarts = pends - pcounts
    dest = (pstarts[se] + jnp.arange(T * TOP_K) - starts[se]).astype(jnp.int32)
    n_rows = (-(-(T * TOP_K) // tm) + E) * tm
    n_blocks = n_rows // tm
    row_tok = jnp.full((n_rows,), T, dtype=jnp.int32).at[dest].set(flat_t[order])
    row_gate = jnp.zeros((n_rows,), F32).at[dest].set(flat_g[order])
    blk_e = jnp.minimum(jnp.searchsorted(pends, jnp.arange(n_blocks) * tm, side='right'), E - 1).astype(jnp.int32)
    n_used = (pends[-1] // tm).astype(jnp.int32).reshape(1)
    h_pad = jnp.concatenate([h2, jnp.zeros((1, D), h2.dtype)], axis=0)
    xb = h_pad[row_tok]
    yb = moe_experts(blk_e, n_used, xb, row_gate, w_gu, b_gu, w_down, b_down, tm)
    pos = jnp.zeros((T * TOP_K,), jnp.int32).at[order].set(dest).reshape(T, TOP_K)
    return jnp.sum(yb[pos], axis=1)


def _gnorm(x, g):
    return x * lax.rsqrt(jnp.mean(x * x, axis=-1, keepdims=True) + EPS) * g


def _rope_tables(pos, half):
    inv_freq = ROPE_THETA ** (-jnp.arange(half, dtype=F32) / half)
    ang = pos.astype(F32)[:, None] * inv_freq[None, :]
    return jnp.cos(ang), jnp.sin(ang)


def _rope(x, cos, sin):
    half = x.shape[-1] // 2
    shp = (cos.shape[0],) + (1,) * (x.ndim - 3) + (half,)
    c = cos.reshape(shp)
    s = sin.reshape(shp)
    x1, x2 = x[..., :half], x[..., half:]
    return jnp.concatenate([x1 * c - x2 * s, x2 * c + x1 * s], axis=-1)


def _in_layout(n_in_pad):
    sizes = [('dq', DIFF_HEADS * 2 * DIFF_D), ('dk', 2 * DIFF_D), ('dv', 2 * DIFF_D),
             ('mq', MLSTM_HEADS * MLSTM_DQK), ('mk', MLSTM_HEADS * MLSTM_DQK),
             ('mv', MLSTM_HEADS * MLSTM_DV), ('mo', MLSTM_HEADS * MLSTM_DV),
             ('aq', MLA_Q_RANK), ('akv', MLA_KV_RANK), ('ar', MLA_ROPE),
             ('mi', MLSTM_HEADS), ('mf', MLSTM_HEADS)]
    out, off = {}, 0
    for name, sz in sizes:
        out[name] = (off, off + sz)
        off += sz
    assert off <= n_in_pad
    return out


def _reorder_w_in(w_in, n_pad):
    o_mi = DIFF_HEADS * 2 * DIFF_D + 4 * DIFF_D + 2 * MLSTM_HEADS * MLSTM_DQK + 2 * MLSTM_HEADS * MLSTM_DV
    o_aq = o_mi + 2 * MLSTM_HEADS
    parts = [w_in[..., :o_mi], w_in[..., o_aq:], w_in[..., o_mi:o_aq]]
    n = sum(p.shape[-1] for p in parts)
    parts.append(jnp.zeros(w_in.shape[:-1] + (n_pad - n,), w_in.dtype))
    return jnp.concatenate(parts, axis=-1).astype(BF16)


def _layer(l, x, mods, rows_per_seq, seq_shape, pos, w, lam, lam_init, past, mstate, cfg):
    B, Ts = seq_shape
    T, D = x.shape
    sh1, sc1, g1, sh2, sc2, g2 = mods
    lay = cfg['layout']
    proj = in_proj(x, w['g_norm1'], sc1, sh1, w['w_in'], cfg['tm_in'], cfg['tn_in'], rows_per_seq)
    col = lambda name: proj[:, lay[name][0]:lay[name][1]]
    cos64, sin64 = _rope_tables(pos, DIFF_D // 2)
    cos32, sin32 = _rope_tables(pos, MLA_ROPE // 2)

    dq = _rope(_gnorm(col('dq').reshape(B, Ts, DIFF_HEADS, 2, DIFF_D), w['g_dq']), cos64, sin64)
    dk = _rope(_gnorm(col('dk').reshape(B, Ts, 1, 2, DIFF_D), w['g_dk']), cos64, sin64)
    k_new = dk.reshape(B, Ts, 1, 2 * DIFF_D)
    v_new = col('dv').reshape(B, Ts, 1, 2 * DIFF_D)

    aqn = _gnorm(col('aq'), w['g_aqa'])
    qfull = matmul(aqn, w['w_aqb'], cfg['tm_mm']).reshape(B, Ts, MLA_HEADS, MLA_NOPE + MLA_ROPE)
    q_nope = _gnorm(qfull[..., :MLA_NOPE], w['g_aqn'])
    q_rope = _rope(_gnorm(qfull[..., MLA_NOPE:], w['g_aqr']), cos32, sin32)
    ckv = _gnorm(col('akv'), w['g_akva']).reshape(B, Ts, MLA_KV_RANK)
    kr = _rope(_gnorm(col('ar'), w['g_akr']).reshape(B, Ts, MLA_ROPE), cos32, sin32)

    if past is None:
        qd = jnp.transpose(dq * DIFF_SCALE, (0, 3, 2, 1, 4)).reshape(B, 2 * DIFF_HEADS, Ts, DIFF_D).astype(BF16)
        kd = jnp.transpose(dk[:, :, 0], (0, 2, 1, 3)).astype(BF16)
        vd = jnp.transpose(v_new, (0, 2, 1, 3)).astype(BF16)
        od = flash_attention(qd, kd, vd, lambda h: h // DIFF_HEADS, lambda h: 0, cfg['t_attn'])
        o1 = od[:, :DIFF_HEADS]
        o2 = od[:, DIFF_HEADS:]
        o_diff = jnp.transpose(o1 - lam * o2, (0, 2, 1, 3))

        kv = matmul(ckv.reshape(T, MLA_KV_RANK), w['w_akvb'], cfg['tm_mm']).reshape(
            B, Ts, MLA_HEADS, MLA_NOPE + MLA_DV)
        k_nope = _gnorm(kv[..., :MLA_NOPE], w['g_akn'])
        qa = jnp.transpose(q_nope * MLA_SCALE, (0, 2, 1, 3)).astype(BF16)
        qra = jnp.transpose(q_rope * MLA_SCALE, (0, 2, 1, 3)).astype(BF16)
        ka = jnp.transpose(k_nope, (0, 2, 1, 3)).astype(BF16)
        va = jnp.transpose(kv[..., MLA_NOPE:], (0, 2, 1, 3)).astype(BF16)
        kra = kr[:, None].astype(BF16)
        oa = flash_attention(qa, ka, va, lambda h: h, lambda h: h, cfg['t_attn'], qr=qra, kr=kra)
        y_a = jnp.transpose(oa, (0, 2, 1, 3)).reshape(B, Ts, MLA_HEADS * MLA_DV)
    else:
        ck, cv, cc, cr, page_table = past
        q1 = jnp.transpose(dq[:, :, :, 0] * DIFF_SCALE, (0, 2, 1, 3)).reshape(B, DIFF_HEADS * Ts, DIFF_D)
        q2 = jnp.transpose(dq[:, :, :, 1] * DIFF_SCALE, (0, 2, 1, 3)).reshape(B, DIFF_HEADS * Ts, DIFF_D)
        z = jnp.zeros_like(q1)
        qd = jnp.concatenate([jnp.concatenate([q1, z], axis=-1), jnp.concatenate([z, q2], axis=-1)],
                             axis=1).astype(BF16)
        qn = jnp.transpose(q_nope * (w['g_akn'] * MLA_SCALE), (0, 2, 1, 3))
        qn = jnp.einsum('bhtd,hg->bhtgd', qn, jnp.eye(MLA_HEADS, dtype=F32)).reshape(
            B, MLA_HEADS * Ts, MLA_HEADS * MLA_NOPE).astype(BF16)
        qrs = jnp.transpose(q_rope * MLA_SCALE, (0, 2, 1, 3)).reshape(B, MLA_HEADS * Ts, MLA_ROPE).astype(BF16)
        od, oa = sample_attention(l, page_table, lam.reshape(1, 1), qd, k_new[:, :, 0], v_new[:, :, 0],
                                  qn, qrs, w['wkt'], w['wv'], ckv, kr, ck, cv, cc, cr)
        o_diff = jnp.transpose(od.reshape(B, DIFF_HEADS, Ts, 2 * DIFF_D), (0, 2, 1, 3))
        y_a = jnp.transpose(oa.reshape(B, MLA_HEADS, Ts, MLA_DV), (0, 2, 1, 3)).reshape(B, Ts, MLA_HEADS * MLA_DV)

    y_d = (_gnorm(o_diff, w['g_dsub']) * (1.0 - lam_init)).reshape(B, Ts, DIFF_HEADS * 2 * DIFF_D)

    ig = (col('mi') + w['b_mi']).reshape(B, Ts, MLSTM_HEADS)
    fg = (col('mf') + w['b_mf']).reshape(B, Ts, MLSTM_HEADS)
    y_m, mc, mn, mm = mlstm(col('mq').reshape(B, Ts, -1).astype(BF16), col('mk').reshape(B, Ts, -1).astype(BF16),
                            col('mv').reshape(B, Ts, -1).astype(BF16), col('mo').reshape(B, Ts, -1),
                            ig, fg, w['g_mout'], mstate[0], mstate[1], mstate[2], math.gcd(Ts, cfg['mlstm_chunk']))

    mix = jnp.concatenate([y_d.astype(BF16), y_m, y_a.astype(BF16)], axis=-1).reshape(T, -1)
    x, h2, logits = out_proj(mix, x, g1, w['w_out'], w['g_norm2'], sc2, sh2, w['w_router'], w['b_router'],
                             cfg['tm_out'], rows_per_seq)
    state = (k_new, v_new, ckv, kr, mc, mn, mm)
    return x, h2, logits, state


def kernel(x_prompt, x_sample, c_prompt, c_sample, cache_diff_k, cache_diff_v, cache_mla_ckv, cache_mla_kr, state_mlstm_c, state_mlstm_n, state_mlstm_m, page_table, w_ada, b_ada, g_norm1, g_norm2, w_in, g_dq, g_dk, lam_diff, g_dsub, b_mi, b_mf, g_mout, g_aqa, w_aqb, g_akva, w_akvb, g_aqn, g_aqr, g_akn, g_akr, w_out, w_router, b_router, w_gu, b_gu, w_down, b_down):
    Bp, Tp, D = x_prompt.shape
    Bs, Ts, _ = x_sample.shape
    depth = w_in.shape[0]
    n_in_pad = -(-w_in.shape[2] // (5 * LANES)) * (5 * LANES)
    cfg = dict(layout=_in_layout(n_in_pad), tm_in=512, tn_in=n_in_pad // 5, tm_mm=512, tm_out=256,
               t_attn=512, mlstm_chunk=256, tm_moe=256)

    n_c = Bp + Bs
    c_all = jnp.concatenate([c_prompt, c_sample, jnp.zeros((-n_c % SUBLANES, D), F32)], axis=0)
    mod = ada_mod(c_all, w_ada, b_ada)[:, :n_c].reshape(depth, n_c, N_MOD, D)
    w_in_r = _reorder_w_in(w_in, n_in_pad)
    w_out_b = w_out.astype(BF16)
    w_aqb_b = w_aqb.astype(BF16)
    w_akvb_b = w_akvb.astype(BF16)
    w_gu_b = w_gu.astype(BF16)
    w_down_b = w_down.astype(BF16)
    kvb = w_akvb.reshape(depth, MLA_KV_RANK, MLA_HEADS, MLA_NOPE + MLA_DV)
    wkt_all = jnp.transpose(kvb[..., :MLA_NOPE], (0, 2, 3, 1)).reshape(
        depth, MLA_HEADS * MLA_NOPE, MLA_KV_RANK).astype(BF16)
    wv_all = kvb[..., MLA_NOPE:].reshape(depth, MLA_KV_RANK, MLA_HEADS * MLA_DV).astype(BF16)

    n_pool = cache_diff_k.shape[1]
    page = cache_diff_k.shape[2]
    ck = cache_diff_k.reshape(depth, n_pool, page, 2 * DIFF_D)
    cv = cache_diff_v.reshape(depth, n_pool, page, 2 * DIFF_D)

    pos_p = jnp.arange(Tp, dtype=jnp.int32)
    pos_s = PAST_LEN + jnp.arange(Ts, dtype=jnp.int32)
    zero_state = (jnp.zeros((Bp, MLSTM_HEADS, MLSTM_DV, MLSTM_DQK), F32),
                  jnp.zeros((Bp, MLSTM_HEADS, MLSTM_DQK), F32),
                  jnp.zeros((Bp, MLSTM_HEADS), F32))

    xp = x_prompt.reshape(Bp * Tp, D)
    xs = x_sample.reshape(Bs * Ts, D)
    new_p = [[] for _ in range(7)]
    new_s = [[] for _ in range(7)]
    for l in range(depth):
        w = dict(g_norm1=g_norm1[l], g_norm2=g_norm2[l], w_in=w_in_r[l], g_dq=g_dq[l], g_dk=g_dk[l],
                 g_dsub=g_dsub[l], b_mi=b_mi[l], b_mf=b_mf[l], g_mout=g_mout[l], g_aqa=g_aqa[l],
                 w_aqb=w_aqb_b[l], g_akva=g_akva[l], w_akvb=w_akvb_b[l], g_aqn=g_aqn[l], g_aqr=g_aqr[l],
                 g_akn=g_akn[l], g_akr=g_akr[l], w_out=w_out_b[l], w_router=w_router[l], b_router=b_router[l],
                 wkt=wkt_all[l], wv=wv_all[l])
        lam_init = 0.8 - 0.6 * math.exp(-0.3 * l)
        lv = lam_diff[l]
        lam = jnp.exp(jnp.sum(lv[0] * lv[1])) - jnp.exp(jnp.sum(lv[2] * lv[3])) + lam_init
        mods_p = tuple(mod[l, :Bp, i][:, None, :] for i in range(N_MOD))
        mods_s = tuple(jnp.repeat(mod[l, Bp:, i], Ts, axis=0) for i in range(N_MOD))

        xp, h2p, lgp, st_p = _layer(l, xp, mods_p, Tp, (Bp, Tp), pos_p, w, lam, lam_init, None, zero_state, cfg)
        past = (ck, cv, cache_mla_ckv, cache_mla_kr, page_table)
        xs, h2s, lgs, st_s = _layer(l, xs, mods_s, Ts, (Bs, Ts), pos_s, w, lam, lam_init, past,
                                    (state_mlstm_c[l], state_mlstm_n[l], state_mlstm_m[l]), cfg)

        y = moe(jnp.concatenate([h2p, h2s], axis=0), jnp.concatenate([lgp, lgs], axis=0),
                w_gu_b[l], b_gu[l], w_down_b[l], b_down[l], cfg['tm_moe'])
        g2p = jnp.broadcast_to(mods_p[5], (Bp, Tp, D)).reshape(Bp * Tp, D)
        xp = xp + g2p * y[:Bp * Tp]
        xs = xs + mods_s[5] * y[Bp * Tp:]
        for lst, a in zip(new_p, st_p):
            lst.append(a)
        for lst, a in zip(new_s, st_s):
            lst.append(a)

    outs_p = [jnp.stack(a) for a in new_p]
    outs_s = [jnp.stack(a) for a in new_s]
    return (xp.reshape(Bp, Tp, D), xs.reshape(Bs, Ts, D), *outs_p, *outs_s)
```

```python
import functools
import math

import jax
import jax.numpy as jnp
from jax import lax
from jax.experimental import pallas as pl
from jax.experimental.pallas import tpu as pltpu

F32 = jnp.float32
BF16 = jnp.bfloat16

PAST_LEN = 8192
DIFF_HEADS = 4
DIFF_D = 64
MLSTM_HEADS = 4
MLSTM_DQK = 128
MLSTM_DV = 256
MLA_HEADS = 4
MLA_Q_RANK = 384
MLA_KV_RANK = 128
MLA_NOPE = 128
MLA_ROPE = 32
MLA_DV = 128
N_EXPERTS = 32
TOP_K = 4
SWIGLU_LIMIT = 7.0
SWIGLU_ALPHA = 1.702
ROPE_THETA = 10000.0
EPS = 1e-6
N_MOD = 6
DIFF_SCALE = DIFF_D ** -0.5
MLA_SCALE = (MLA_NOPE + MLA_ROPE) ** -0.5

LANES = 128
SUBLANES = 8
VMEM_LIMIT_BYTES = 56 * 1024 * 1024
NEG_BIG = -1e30

_NT = (((1,), (1,)), ((), ()))
_TN = (((0,), (0,)), ((), ()))


def _params(semantics, vmem=VMEM_LIMIT_BYTES):
    return pltpu.CompilerParams(dimension_semantics=semantics, vmem_limit_bytes=vmem)


def _ada_kernel(c_ref, w_ref, b_ref, o_ref):
    c = c_ref[...]
    a = (c * jax.nn.sigmoid(c)).astype(BF16)
    o_ref[...] = jnp.dot(a, w_ref[...].astype(BF16), preferred_element_type=F32) + b_ref[...]


def ada_mod(c, w_ada, b_ada, tn=1024):
    L, D, N = w_ada.shape
    Bc = c.shape[0]
    tn = math.gcd(N, tn)
    return pl.pallas_call(
        _ada_kernel,
        grid=(L, N // tn),
        in_specs=[pl.BlockSpec((Bc, D), lambda l, j: (0, 0)),
                  pl.BlockSpec((None, D, tn), lambda l, j: (l, 0, j)),
                  pl.BlockSpec((None, 1, tn), lambda l, j: (l, 0, j))],
        out_specs=pl.BlockSpec((None, Bc, tn), lambda l, j: (l, 0, j)),
        out_shape=jax.ShapeDtypeStruct((L, Bc, N), F32),
        compiler_params=_params(("arbitrary", "arbitrary")),
        name="ada_mod",
    )(c, w_ada, b_ada.reshape(L, 1, N))


def _matmul_kernel(x_ref, w_ref, o_ref):
    o_ref[...] = jnp.dot(x_ref[...].astype(BF16), w_ref[...].astype(BF16),
                         preferred_element_type=F32).astype(o_ref.dtype)


def matmul(layer, x, w, tm, out_dtype=F32):
    M, K = x.shape
    N = w.shape[2]
    tm = min(tm, M)
    return pl.pallas_call(
        _matmul_kernel,
        grid=(M // tm,),
        in_specs=[pl.BlockSpec((tm, K), lambda i: (i, 0)),
                  pl.BlockSpec((None, K, N), lambda i: (layer, 0, 0))],
        out_specs=pl.BlockSpec((tm, N), lambda i: (i, 0)),
        out_shape=jax.ShapeDtypeStruct((M, N), out_dtype),
        compiler_params=_params(("arbitrary",)),
        name="matmul",
    )(x, w)


def _rms(x):
    return x * lax.rsqrt(jnp.mean(x * x, axis=-1, keepdims=True) + EPS)


def _inproj_kernel(x_ref, g_ref, sc_ref, sh_ref, w_ref, o_ref, h_scr):
    @pl.when(pl.program_id(1) == 0)
    def _():
        y = _rms(x_ref[...]) * g_ref[...]
        h_scr[...] = (y * (1.0 + sc_ref[...]) + sh_ref[...]).astype(BF16)

    o_ref[...] = jnp.dot(h_scr[...], w_ref[...], preferred_element_type=F32)


def _mod_spec(mod, tm, rows_per_seq):
    if mod.ndim == 3:
        tiles = rows_per_seq // tm
        return pl.BlockSpec((None, 1, mod.shape[-1]), lambda i, *_: (i // tiles, 0, 0))
    return pl.BlockSpec((tm, mod.shape[-1]), lambda i, *_: (i, 0))


def in_proj(layer, x, g, sc, sh, w, tm, tn, rows_per_seq):
    T, D = x.shape
    N = w.shape[2]
    tm = min(tm, T)
    return pl.pallas_call(
        _inproj_kernel,
        grid=(T // tm, N // tn),
        in_specs=[pl.BlockSpec((tm, D), lambda i, j: (i, 0)),
                  pl.BlockSpec((1, D), lambda i, j: (0, 0)),
                  _mod_spec(sc, tm, rows_per_seq),
                  _mod_spec(sh, tm, rows_per_seq),
                  pl.BlockSpec((None, D, tn), lambda i, j: (layer, 0, j))],
        out_specs=pl.BlockSpec((tm, tn), lambda i, j: (i, j)),
        out_shape=jax.ShapeDtypeStruct((T, N), F32),
        scratch_shapes=[pltpu.VMEM((tm, D), BF16)],
        compiler_params=_params(("arbitrary", "arbitrary")),
        name="in_proj",
    )(x, g.reshape(1, D), sc, sh, w)


def _outproj_kernel(mix_ref, x_ref, g1_ref, w_ref, gn_ref, sc_ref, sh_ref, wr_ref, br_ref,
                    xo_ref, h2_ref, lg_ref):
    x = x_ref[...] + g1_ref[...] * jnp.dot(mix_ref[...], w_ref[...], preferred_element_type=F32)
    xo_ref[...] = x
    h2 = (_rms(x) * gn_ref[...]) * (1.0 + sc_ref[...]) + sh_ref[...]
    h2_ref[...] = h2
    lg_ref[...] = jnp.dot(h2, wr_ref[...], precision=lax.Precision.HIGHEST,
                          preferred_element_type=F32) + br_ref[...]


def out_proj(layer, mix, x, g1, w, gn, sc, sh, wr, br, tm, rows_per_seq):
    T, D = x.shape
    K = mix.shape[1]
    E = wr.shape[2]
    tm = min(tm, T)
    return pl.pallas_call(
        _outproj_kernel,
        grid=(T // tm,),
        in_specs=[pl.BlockSpec((tm, K), lambda i: (i, 0)),
                  pl.BlockSpec((tm, D), lambda i: (i, 0)),
                  _mod_spec(g1, tm, rows_per_seq),
                  pl.BlockSpec((None, K, D), lambda i: (layer, 0, 0)),
                  pl.BlockSpec((1, D), lambda i: (0, 0)),
                  _mod_spec(sc, tm, rows_per_seq),
                  _mod_spec(sh, tm, rows_per_seq),
                  pl.BlockSpec((None, D, E), lambda i: (layer, 0, 0)),
                  pl.BlockSpec((1, E), lambda i: (0, 0))],
        out_specs=[pl.BlockSpec((tm, D), lambda i: (i, 0)),
                   pl.BlockSpec((tm, D), lambda i: (i, 0)),
                   pl.BlockSpec((tm, E), lambda i: (i, 0))],
        out_shape=[jax.ShapeDtypeStruct((T, D), F32),
                   jax.ShapeDtypeStruct((T, D), F32),
                   jax.ShapeDtypeStruct((T, E), F32)],
        compiler_params=_params(("arbitrary",)),
        name="out_proj",
    )(mix, x, g1, w, gn.reshape(1, D), sc, sh, wr, br.reshape(1, E))


def _flash_kernel(*refs, has_rope, t):
    if has_rope:
        q_ref, k_ref, v_ref, qr_ref, kr_ref, o_ref = refs
        qr = qr_ref[...]
    else:
        q_ref, k_ref, v_ref, o_ref = refs
    i = pl.program_id(2)
    q = q_ref[...]
    dv = v_ref.shape[-1]

    def block(off, masked, carry):
        m_prev, l_prev, acc = carry
        s = lax.dot_general(q, k_ref[pl.ds(off, t), :], _NT, preferred_element_type=F32)
        if has_rope:
            s = s + lax.dot_general(qr, kr_ref[pl.ds(off, t), :], _NT, preferred_element_type=F32)
        if masked:
            row = lax.broadcasted_iota(jnp.int32, s.shape, 0)
            col = lax.broadcasted_iota(jnp.int32, s.shape, 1)
            s = jnp.where(col <= row, s, NEG_BIG)
        m_new = jnp.maximum(m_prev, jnp.max(s, axis=1, keepdims=True))
        alpha = jnp.exp(m_prev - m_new)
        p = jnp.exp(s - m_new)
        return (m_new, alpha * l_prev + jnp.sum(p, axis=1, keepdims=True),
                alpha * acc + jnp.dot(p.astype(BF16), v_ref[pl.ds(off, t), :], preferred_element_type=F32))

    init = (jnp.full((t, 1), NEG_BIG, F32), jnp.zeros((t, 1), F32), jnp.zeros((t, dv), F32))
    carry = lax.fori_loop(0, i, lambda j, c: block(pl.multiple_of(j * t, t), False, c), init)
    _, l, acc = block(pl.multiple_of(i * t, t), True, carry)
    o_ref[...] = acc / l


def flash_attention(q, k, v, kmap, vmap, t, qr=None, kr=None):
    B, H, T, dk = q.shape
    dv = v.shape[-1]
    in_specs = [pl.BlockSpec((None, None, t, dk), lambda b, h, i: (b, h, i, 0)),
                pl.BlockSpec((None, None, T, dk), lambda b, h, i: (b, kmap(h), 0, 0)),
                pl.BlockSpec((None, None, T, dv), lambda b, h, i: (b, vmap(h), 0, 0))]
    args = [q, k, v]
    if qr is not None:
        dr = qr.shape[-1]
        in_specs += [pl.BlockSpec((None, None, t, dr), lambda b, h, i: (b, h, i, 0)),
                     pl.BlockSpec((None, None, T, dr), lambda b, h, i: (b, 0, 0, 0))]
        args += [qr, kr]
    return pl.pallas_call(
        functools.partial(_flash_kernel, has_rope=qr is not None, t=t),
        grid=(B, H, T // t),
        in_specs=in_specs,
        out_specs=pl.BlockSpec((None, None, t, dv), lambda b, h, i: (b, h, i, 0)),
        out_shape=jax.ShapeDtypeStruct((B, H, T, dv), F32),
        compiler_params=_params(("arbitrary",) * 3),
        name="flash_attention",
    )(*args)


N_PAGED = 4


def _sample_attn_kernel(pt_ref, lam_ref, qd_ref, knew_ref, vnew_ref, qn_ref, qr_ref, wkt_ref, wv_ref,
                        cnew_ref, rnew_ref, kc_hbm, vc_hbm, cc_hbm, rc_hbm, od_ref, oa_ref,
                        kbuf, vbuf, cbuf, rbuf, sem, s_scr, *, layer, n_pages, page, chunk, ts):
    b = pl.program_id(0)
    nb = pl.num_programs(0)
    slot = b % 2
    tp = n_pages * page
    n_chunks = tp // chunk
    hd = MLA_HEADS
    rd = 2 * DIFF_HEADS * ts
    ra = MLA_HEADS * ts
    hbm = (kc_hbm, vc_hbm, cc_hbm, rc_hbm)
    bufs = (kbuf, vbuf, cbuf, rbuf)

    def tokens(buf, sl, start, size):
        if buf is rbuf:
            return buf.at[sl, :, pl.ds(start, size)]
        return buf.at[sl, pl.ds(start, size)]

    def start_gather(seq, sl):
        def body(p, carry):
            pg = pt_ref[seq * n_pages + p]
            for a in range(N_PAGED):
                pltpu.make_async_copy(hbm[a].at[layer, pg],
                                      tokens(bufs[a], sl, pl.multiple_of(p * page, page), page),
                                      sem.at[a, sl]).start()
            return carry
        lax.fori_loop(0, n_pages, body, 0)

    def wait_gather(sl):
        for a in range(N_PAGED):
            past = tokens(bufs[a], sl, 0, tp)
            pltpu.make_async_copy(past, past, sem.at[a, sl]).wait()

    @pl.when(b == 0)
    def _():
        for sl in range(2):
            for buf in (kbuf, vbuf, cbuf):
                buf[sl, tp:tp + page, :] = jnp.zeros((page, buf.shape[-1]), F32)
            rbuf[sl, :, tp:tp + page] = jnp.zeros((rbuf.shape[1], page), F32)
        start_gather(0, 0)

    @pl.when(b + 1 < nb)
    def _():
        start_gather(b + 1, 1 - slot)

    kbuf[slot, tp:tp + ts, :] = knew_ref[...]
    vbuf[slot, tp:tp + ts, :] = vnew_ref[...]
    cbuf[slot, tp:tp + ts, :] = cnew_ref[...]
    rbuf[slot, :, tp:tp + ts] = rnew_ref[...]

    qd = qd_ref[...]
    qr = qr_ref[...]
    wkt = wkt_ref[...]
    qabs = jnp.dot(qn_ref[...], wkt, preferred_element_type=F32).astype(BF16)
    lhs = jnp.concatenate([wkt, qabs], axis=0)

    def scores(off, n):
        kc = kbuf[slot, pl.ds(off, n), :].astype(BF16)
        cc = cbuf[slot, pl.ds(off, n), :].astype(BF16)
        rc = rbuf[slot, :, pl.ds(off, n)].astype(BF16)
        sd = lax.dot_general(qd, kc, _NT, preferred_element_type=F32)
        big = lax.dot_general(lhs, cc, _NT, preferred_element_type=F32)
        sr = jnp.dot(qr, rc, preferred_element_type=F32)
        parts = [sd]
        for h in range(hd):
            kn = big[h * MLA_NOPE:(h + 1) * MLA_NOPE, :]
            ss = jnp.sum(kn * kn, axis=0, keepdims=True)
            rinv = lax.rsqrt(ss * (1.0 / MLA_NOPE) + EPS)
            lo = hd * MLA_NOPE + h * ts
            parts.append(big[lo:lo + ts, :] * rinv + sr[h * ts:(h + 1) * ts, :])
        return jnp.concatenate(parts, axis=0)

    def weighted(p, off, n):
        pb = p.astype(BF16)
        vc = vbuf[slot, pl.ds(off, n), :].astype(BF16)
        cc = cbuf[slot, pl.ds(off, n), :].astype(BF16)
        return (jnp.dot(pb[:rd], vc, preferred_element_type=F32),
                jnp.dot(pb[rd:], cc, preferred_element_type=F32))

    s_tail = scores(tp, page)
    row = lax.broadcasted_iota(jnp.int32, s_tail.shape, 0)
    col = lax.broadcasted_iota(jnp.int32, s_tail.shape, 1)
    s_tail = jnp.where(col <= row % ts, s_tail, NEG_BIG)
    m0 = jnp.max(s_tail, axis=1, keepdims=True)

    wait_gather(slot)

    def pass1(c, m):
        s = scores(pl.multiple_of(c * chunk, chunk), chunk)
        s_scr[c] = s
        return jnp.maximum(m, jnp.max(s, axis=1, keepdims=True))

    m = lax.fori_loop(0, n_chunks, pass1, m0)

    p_tail = jnp.exp(s_tail - m)
    l0 = jnp.sum(p_tail, axis=1, keepdims=True)
    accd0, acca0 = weighted(p_tail, tp, page)

    def pass2(c, carry):
        l, accd, acca = carry
        p = jnp.exp(s_scr[c] - m)
        dd, da = weighted(p, pl.multiple_of(c * chunk, chunk), chunk)
        return l + jnp.sum(p, axis=1, keepdims=True), accd + dd, acca + da

    l, accd, acca = lax.fori_loop(0, n_chunks, pass2, (l0, accd0, acca0))

    on = accd / l[:rd]
    half = rd // 2
    od_ref[...] = on[:half] - lam_ref[0, 0] * on[half:]
    lat = (acca / l[rd:]).astype(BF16)
    full = jnp.dot(lat, wv_ref[...], preferred_element_type=F32)
    oa_ref[...] = jnp.concatenate(
        [full[h * ts:(h + 1) * ts, h * MLA_DV:(h + 1) * MLA_DV] for h in range(hd)], axis=0)


def sample_attention(layer, page_table, lam, qd, knew, vnew, qn, qr, wkt, wv, cnew, rnew,
                     cache_k, cache_v, cache_c, cache_r, chunk=1024):
    DB, n_pages = page_table.shape
    page = cache_k.shape[2]
    ts = knew.shape[1]
    rd = 2 * DIFF_HEADS * ts
    ra = MLA_HEADS * ts
    tp = n_pages * page
    chunk = min(chunk, tp)
    kern = functools.partial(_sample_attn_kernel, layer=layer, n_pages=n_pages, page=page, chunk=chunk, ts=ts)
    seq3 = lambda b, pt: (b, 0, 0)
    wl = lambda b, pt: (layer, 0, 0)
    any_spec = pl.BlockSpec(memory_space=pl.ANY)
    grid_spec = pltpu.PrefetchScalarGridSpec(
        num_scalar_prefetch=1,
        grid=(DB,),
        in_specs=[pl.BlockSpec(memory_space=pltpu.SMEM),
                  pl.BlockSpec((None, rd, 2 * DIFF_D), seq3),
                  pl.BlockSpec((None, ts, 2 * DIFF_D), seq3),
                  pl.BlockSpec((None, ts, 2 * DIFF_D), seq3),
                  pl.BlockSpec((None, ra, MLA_HEADS * MLA_NOPE), seq3),
                  pl.BlockSpec((None, ra, MLA_ROPE), seq3),
                  pl.BlockSpec((None, MLA_HEADS * MLA_NOPE, MLA_KV_RANK), wl),
                  pl.BlockSpec((None, MLA_KV_RANK, MLA_HEADS * MLA_DV), wl),
                  pl.BlockSpec((None, ts, MLA_KV_RANK), seq3),
                  pl.BlockSpec((None, MLA_ROPE, ts), seq3),
                  any_spec, any_spec, any_spec, any_spec],
        out_specs=[pl.BlockSpec((None, rd // 2, 2 * DIFF_D), seq3),
                   pl.BlockSpec((None, ra, MLA_DV), seq3)],
        scratch_shapes=[pltpu.VMEM((2, tp + page, 2 * DIFF_D), F32),
                        pltpu.VMEM((2, tp + page, 2 * DIFF_D), F32),
                        pltpu.VMEM((2, tp + page, MLA_KV_RANK), F32),
                        pltpu.VMEM((2, MLA_ROPE, tp + page), F32),
                        pltpu.SemaphoreType.DMA((N_PAGED, 2)),
                        pltpu.VMEM((tp // chunk, rd + ra, chunk), F32)],
    )
    return pl.pallas_call(
        kern,
        grid_spec=grid_spec,
        out_shape=[jax.ShapeDtypeStruct((DB, rd // 2, 2 * DIFF_D), F32),
                   jax.ShapeDtypeStruct((DB, ra, MLA_DV), F32)],
        compiler_params=_params(("arbitrary",)),
        name="sample_attention",
    )(page_table.reshape(-1), lam, qd, knew, vnew, qn, qr, wkt, wv, cnew, rnew,
      cache_k, cache_v, cache_c, cache_r)


def _log_sigmoid(x):
    return jnp.minimum(x, 0.0) - jnp.log(1.0 + jnp.exp(-jnp.abs(x)))


def _mlstm_kernel(q_ref, k_ref, v_ref, o_ref, icol_ref, irow_ref, fcol_ref, frow_ref, g_ref,
                  c0_ref, n0_ref, m0_ref, y_ref, c_ref, n_ref, m_ref, *, L):
    ci = pl.program_id(1)
    DQK, DV = MLSTM_DQK, MLSTM_DV

    @pl.when(ci == 0)
    def _():
        c_ref[...] = c0_ref[...]
        n_ref[...] = n0_ref[...]
        m_ref[...] = m0_ref[...]

    t_idx = lax.broadcasted_iota(jnp.int32, (L, L), 0)
    s_idx = lax.broadcasted_iota(jnp.int32, (L, L), 1)
    causal = s_idx <= t_idx
    k_scale = DQK ** -0.5

    for h in range(MLSTM_HEADS):
        q = q_ref[:, h * DQK:(h + 1) * DQK].astype(BF16)
        k = k_ref[:, h * DQK:(h + 1) * DQK].astype(BF16)
        v = v_ref[:, h * DV:(h + 1) * DV].astype(BF16)
        i_col = icol_ref[h]
        i_row = irow_ref[h]
        f_col = _log_sigmoid(fcol_ref[h])
        f_row = _log_sigmoid(frow_ref[h])
        C = c_ref[h]
        n = n_ref[h]
        m_prev = m_ref[h]

        b_col = jnp.sum(jnp.where(causal, f_row, 0.0), axis=1, keepdims=True)
        b_row = jnp.sum(jnp.where(t_idx <= s_idx, f_col, 0.0), axis=0, keepdims=True)
        b_end = b_col[L - 1:L, :]

        logw = jnp.where(causal, b_col - b_row + i_row, -jnp.inf)
        inter = b_col + m_prev
        m_t = jnp.maximum(inter, jnp.max(logw, axis=1, keepdims=True))
        qk = lax.dot_general(q, k, _NT, preferred_element_type=F32) * k_scale
        s = jnp.exp(logw - m_t) * qk
        a = jnp.exp(inter - m_t)
        num = jnp.dot(s.astype(BF16), v, preferred_element_type=F32) \
            + a * lax.dot_general(q, C.astype(BF16), _NT, preferred_element_type=F32)
        den = jnp.sum(s, axis=1, keepdims=True) + a * jnp.sum(q.astype(F32) * n, axis=1, keepdims=True)
        hh = num / jnp.maximum(jnp.abs(den), jnp.exp(-m_t))

        m_new = m_t[L - 1:L, :]
        w_end = jnp.exp(b_end - b_col + i_col - m_new)
        decay = jnp.exp(b_end + m_prev - m_new)
        wv = (w_end * v.astype(F32)).astype(BF16)
        c_ref[h] = decay * C + k_scale * lax.dot_general(wv, k, _TN, preferred_element_type=F32)
        n_ref[h] = decay * n + k_scale * jnp.sum(w_end * k.astype(F32), axis=0, keepdims=True)
        m_ref[h] = m_new

        hn = _rms(hh) * g_ref[:, h * DV:(h + 1) * DV]
        y_ref[:, h * DV:(h + 1) * DV] = (jax.nn.sigmoid(o_ref[:, h * DV:(h + 1) * DV]) * hn).astype(y_ref.dtype)


def mlstm(layer, proj, lay, ig, fg, g, c0, n0, m0, L):
    B, T, _ = proj.shape
    H, DQK, DV = MLSTM_HEADS, MLSTM_DQK, MLSTM_DV
    Ls = c0.shape[0]
    nc = T // L

    def cols(name, width):
        start, end = lay[name]
        assert end - start == width and start % width == 0
        return pl.BlockSpec((None, L, width), lambda b, c: (b, c, start // width))

    icol = jnp.transpose(ig, (0, 2, 1))[..., None]
    irow = jnp.transpose(ig, (0, 2, 1))[:, :, None, :]
    fcol = jnp.transpose(fg, (0, 2, 1))[..., None]
    frow = jnp.transpose(fg, (0, 2, 1))[:, :, None, :]
    tok = lambda b, c: (b, c, 0)
    col = lambda b, c: (b, 0, c, 0)
    rowm = lambda b, c: (b, 0, 0, c)
    st0 = lambda b, c: (layer, b, 0, 0, 0)
    st = lambda b, c: (b, 0, 0, 0)
    y, c, n, m = pl.pallas_call(
        functools.partial(_mlstm_kernel, L=L),
        grid=(B, nc),
        in_specs=[cols('mq', H * DQK), cols('mk', H * DQK), cols('mv', H * DV), cols('mo', H * DV),
                  pl.BlockSpec((None, H, L, 1), col),
                  pl.BlockSpec((None, H, 1, L), rowm),
                  pl.BlockSpec((None, H, L, 1), col),
                  pl.BlockSpec((None, H, 1, L), rowm),
                  pl.BlockSpec((1, H * DV), lambda b, c: (0, 0)),
                  pl.BlockSpec((None, None, H, DV, DQK), st0),
                  pl.BlockSpec((None, None, H, 1, DQK), st0),
                  pl.BlockSpec((None, None, H, 1, 1), st0)],
        out_specs=[pl.BlockSpec((None, L, H * DV), tok),
                   pl.BlockSpec((None, H, DV, DQK), st),
                   pl.BlockSpec((None, H, 1, DQK), st),
                   pl.BlockSpec((None, H, 1, 1), st)],
        out_shape=[jax.ShapeDtypeStruct((B, T, H * DV), BF16),
                   jax.ShapeDtypeStruct((B, H, DV, DQK), F32),
                   jax.ShapeDtypeStruct((B, H, 1, DQK), F32),
                   jax.ShapeDtypeStruct((B, H, 1, 1), F32)],
        compiler_params=_params(("arbitrary",) * 2),
        name="mlstm",
    )(proj, proj, proj, proj, icol, irow, fcol, frow, g.reshape(1, H * DV),
      c0, n0.reshape(Ls, B, H, 1, DQK), m0.reshape(Ls, B, H, 1, 1))
    return y, c, n.reshape(B, H, DQK), m.reshape(B, H)


def _moe_kernel(be_ref, nu_ref, x_ref, gate_ref, wgu_ref, bgu_ref, wd_ref, bd_ref, o_ref, *, f):
    blk = pl.program_id(0)

    @pl.when(blk < nu_ref[0])
    def _():
        gu = jnp.dot(x_ref[...].astype(BF16), wgu_ref[...], preferred_element_type=F32) + bgu_ref[...]
        g_ = jnp.minimum(gu[:, :f], SWIGLU_LIMIT)
        u_ = jnp.clip(gu[:, f:], -SWIGLU_LIMIT, SWIGLU_LIMIT)
        act = g_ * jax.nn.sigmoid(SWIGLU_ALPHA * g_) * (u_ + 1.0)
        y = jnp.dot(act.astype(BF16), wd_ref[...], preferred_element_type=F32) + bd_ref[...]
        o_ref[...] = y * gate_ref[...]

    @pl.when(blk >= nu_ref[0])
    def _():
        o_ref[...] = jnp.zeros(o_ref.shape, F32)


def moe_experts(layer, blk_e, n_used, xb, row_gate, w_gu, b_gu, w_down, b_down, tm):
    R, D = xb.shape
    L, E, _, F2 = w_gu.shape
    f = F2 // 2
    wmap = lambda i, be, nu: (layer, be[i], 0, 0)
    grid_spec = pltpu.PrefetchScalarGridSpec(
        num_scalar_prefetch=2,
        grid=(R // tm,),
        in_specs=[pl.BlockSpec((tm, D), lambda i, be, nu: (i, 0)),
                  pl.BlockSpec((tm, 1), lambda i, be, nu: (i, 0)),
                  pl.BlockSpec((None, None, D, F2), wmap),
                  pl.BlockSpec((None, None, 1, F2), wmap),
                  pl.BlockSpec((None, None, f, D), wmap),
                  pl.BlockSpec((None, None, 1, D), wmap)],
        out_specs=pl.BlockSpec((tm, D), lambda i, be, nu: (i, 0)),
    )
    return pl.pallas_call(
        functools.partial(_moe_kernel, f=f),
        grid_spec=grid_spec,
        out_shape=jax.ShapeDtypeStruct((R, D), F32),
        compiler_params=_params(("arbitrary",)),
        name="moe_experts",
    )(blk_e, n_used, xb, row_gate.reshape(R, 1), w_gu, b_gu.reshape(L, E, 1, F2), w_down,
      b_down.reshape(L, E, 1, D))


def moe(layer, h2, logits, w_gu, b_gu, w_down, b_down, tm):
    T, D = h2.shape
    E = N_EXPERTS
    A = T * TOP_K
    top_v, top_e = lax.top_k(logits, TOP_K)
    gate = jax.nn.softmax(top_v, axis=-1)
    flat_e = top_e.reshape(-1)
    flat_g = gate.reshape(-1)
    onehot = flat_e[:, None] == jnp.arange(E, dtype=flat_e.dtype)[None, :]
    running = jnp.cumsum(onehot.astype(jnp.int32), axis=0)
    counts = running[-1]
    rank = jnp.sum(jnp.where(onehot, running, 0), axis=1) - 1
    pcounts = (counts + tm - 1) // tm * tm
    starts = jnp.cumsum(counts) - counts
    pends = jnp.cumsum(pcounts)
    pstarts = pends - pcounts
    pos = jnp.sum(jnp.where(onehot, pstarts[None, :], 0), axis=1) + rank
    order = jnp.argsort(flat_e)
    n_rows = (-(-A // tm) + E) * tm
    n_blocks = n_rows // tm
    blk_start = jnp.arange(n_blocks, dtype=jnp.int32) * tm
    blk_e = jnp.minimum(jnp.sum(pends[None, :] <= blk_start[:, None], axis=1), E - 1).astype(jnp.int32)
    n_used = (pends[-1] // tm).astype(jnp.int32).reshape(1)
    e_row = jnp.repeat(blk_e, tm)
    off = jnp.arange(n_rows, dtype=jnp.int32) - pstarts[e_row]
    valid = off < counts[e_row]
    src = order[jnp.clip(starts[e_row] + off, 0, A - 1)]
    row_tok = jnp.where(valid, src // TOP_K, T).astype(jnp.int32)
    row_gate = jnp.where(valid, flat_g[src], 0.0)
    h_pad = jnp.concatenate([h2, jnp.zeros((1, D), h2.dtype)], axis=0)
    xb = h_pad[row_tok]
    yb = moe_experts(layer, blk_e, n_used, xb, row_gate, w_gu, b_gu, w_down, b_down, tm)
    pos_km = jnp.transpose(pos.reshape(T, TOP_K)).reshape(-1)
    return jnp.sum(yb[pos_km].reshape(TOP_K, T, D), axis=0)


def _gnorm(x, g):
    return x * lax.rsqrt(jnp.mean(x * x, axis=-1, keepdims=True) + EPS) * g


def _rope_tables(pos, half):
    inv_freq = ROPE_THETA ** (-jnp.arange(half, dtype=F32) / half)
    ang = pos.astype(F32)[:, None] * inv_freq[None, :]
    return jnp.cos(ang), jnp.sin(ang)


def _rope(x, cos, sin):
    half = x.shape[-1] // 2
    shp = (cos.shape[0],) + (1,) * (x.ndim - 3) + (half,)
    c = cos.reshape(shp)
    s = sin.reshape(shp)
    x1, x2 = x[..., :half], x[..., half:]
    return jnp.concatenate([x1 * c - x2 * s, x2 * c + x1 * s], axis=-1)


_W_IN_GIVEN = [('dq', DIFF_HEADS * 2 * DIFF_D), ('dk', 2 * DIFF_D), ('dv', 2 * DIFF_D),
               ('mq', MLSTM_HEADS * MLSTM_DQK), ('mk', MLSTM_HEADS * MLSTM_DQK),
               ('mv', MLSTM_HEADS * MLSTM_DV), ('mo', MLSTM_HEADS * MLSTM_DV),
               ('mi', MLSTM_HEADS), ('mf', MLSTM_HEADS),
               ('aq', MLA_Q_RANK), ('akv', MLA_KV_RANK), ('ar', MLA_ROPE)]
_W_IN_ORDER = ['mv', 'mo', 'mq', 'mk', 'dq', 'dk', 'dv', 'aq', 'akv', 'ar', 'mi', 'mf']


def _offsets(names, sizes):
    out, off = {}, 0
    for name in names:
        out[name] = (off, off + sizes[name])
        off += sizes[name]
    return out


def _in_layout():
    return _offsets(_W_IN_ORDER, dict(_W_IN_GIVEN))


def _reorder_w_in(w_in, n_pad):
    given = _offsets([n for n, _ in _W_IN_GIVEN], dict(_W_IN_GIVEN))
    parts = [w_in[..., given[n][0]:given[n][1]] for n in _W_IN_ORDER]
    parts.append(jnp.zeros(w_in.shape[:-1] + (n_pad - w_in.shape[-1],), w_in.dtype))
    return jnp.concatenate(parts, axis=-1).astype(BF16)


def _layer(l, x, mods, rows_per_seq, seq_shape, pos, w, wl, lam, lam_init, past, mstate, cfg):
    B, Ts = seq_shape
    T, D = x.shape
    sh1, sc1, g1, sh2, sc2, g2 = mods
    lay = cfg['layout']
    proj = in_proj(l, x, w['g_norm1'], sc1, sh1, wl['w_in'], cfg['tm_in'], cfg['tn_in'], rows_per_seq)
    col = lambda name: proj[:, lay[name][0]:lay[name][1]]
    cos64, sin64 = _rope_tables(pos, DIFF_D // 2)
    cos32, sin32 = _rope_tables(pos, MLA_ROPE // 2)

    dq = _rope(_gnorm(col('dq').reshape(B, Ts, DIFF_HEADS, 2, DIFF_D), w['g_dq']), cos64, sin64)
    dk = _rope(_gnorm(col('dk').reshape(B, Ts, 1, 2, DIFF_D), w['g_dk']), cos64, sin64)
    k_new = dk.reshape(B, Ts, 1, 2 * DIFF_D)
    v_new = col('dv').reshape(B, Ts, 1, 2 * DIFF_D)

    aqn = _gnorm(col('aq'), w['g_aqa'])
    qfull = matmul(l, aqn, wl['w_aqb'], cfg['tm_mm']).reshape(B, Ts, MLA_HEADS, MLA_NOPE + MLA_ROPE)
    q_nope = _gnorm(qfull[..., :MLA_NOPE], w['g_aqn'])
    q_rope = _rope(_gnorm(qfull[..., MLA_NOPE:], w['g_aqr']), cos32, sin32)
    ckv = _gnorm(col('akv'), w['g_akva']).reshape(B, Ts, MLA_KV_RANK)
    kr = _rope(_gnorm(col('ar'), w['g_akr']).reshape(B, Ts, MLA_ROPE), cos32, sin32)

    if past is None:
        qd = jnp.transpose(dq * DIFF_SCALE, (0, 3, 2, 1, 4)).reshape(B, 2 * DIFF_HEADS, Ts, DIFF_D).astype(BF16)
        kd = jnp.transpose(dk[:, :, 0], (0, 2, 1, 3)).astype(BF16)
        vd = jnp.transpose(v_new, (0, 2, 1, 3)).astype(BF16)
        od = flash_attention(qd, kd, vd, lambda h: h // DIFF_HEADS, lambda h: 0, cfg['t_attn'])
        o1 = od[:, :DIFF_HEADS]
        o2 = od[:, DIFF_HEADS:]
        o_diff = jnp.transpose(o1 - lam * o2, (0, 2, 1, 3))

        kv = matmul(l, ckv.reshape(T, MLA_KV_RANK), wl['w_akvb'], cfg['tm_mm']).reshape(
            B, Ts, MLA_HEADS, MLA_NOPE + MLA_DV)
        k_nope = _gnorm(kv[..., :MLA_NOPE], w['g_akn'])
        qa = jnp.transpose(q_nope * MLA_SCALE, (0, 2, 1, 3)).astype(BF16)
        qra = jnp.transpose(q_rope * MLA_SCALE, (0, 2, 1, 3)).astype(BF16)
        ka = jnp.transpose(k_nope, (0, 2, 1, 3)).astype(BF16)
        va = jnp.transpose(kv[..., MLA_NOPE:], (0, 2, 1, 3)).astype(BF16)
        kra = kr[:, None].astype(BF16)
        oa = flash_attention(qa, ka, va, lambda h: h, lambda h: h, cfg['t_attn'], qr=qra, kr=kra)
        y_a = jnp.transpose(oa, (0, 2, 1, 3)).reshape(B, Ts, MLA_HEADS * MLA_DV)
    else:
        ck, cv, cc, cr, page_table = past
        q1 = jnp.transpose(dq[:, :, :, 0] * DIFF_SCALE, (0, 2, 1, 3)).reshape(B, DIFF_HEADS * Ts, DIFF_D)
        q2 = jnp.transpose(dq[:, :, :, 1] * DIFF_SCALE, (0, 2, 1, 3)).reshape(B, DIFF_HEADS * Ts, DIFF_D)
        z = jnp.zeros_like(q1)
        qd = jnp.concatenate([jnp.concatenate([q1, z], axis=-1), jnp.concatenate([z, q2], axis=-1)],
                             axis=1).astype(BF16)
        qn = jnp.transpose(q_nope * (w['g_akn'] * MLA_SCALE), (0, 2, 1, 3))
        qn = jnp.einsum('bhtd,hg->bhtgd', qn, jnp.eye(MLA_HEADS, dtype=F32)).reshape(
            B, MLA_HEADS * Ts, MLA_HEADS * MLA_NOPE).astype(BF16)
        qrs = jnp.transpose(q_rope * MLA_SCALE, (0, 2, 1, 3)).reshape(B, MLA_HEADS * Ts, MLA_ROPE).astype(BF16)
        od, oa = sample_attention(l, page_table, lam.reshape(1, 1), qd, k_new[:, :, 0], v_new[:, :, 0],
                                  qn, qrs, wl['wkt'], wl['wv'], ckv, jnp.swapaxes(kr, 1, 2), ck, cv, cc, cr)
        o_diff = jnp.transpose(od.reshape(B, DIFF_HEADS, Ts, 2 * DIFF_D), (0, 2, 1, 3))
        y_a = jnp.transpose(oa.reshape(B, MLA_HEADS, Ts, MLA_DV), (0, 2, 1, 3)).reshape(B, Ts, MLA_HEADS * MLA_DV)

    y_d = (_gnorm(o_diff, w['g_dsub']) * (1.0 - lam_init)).reshape(B, Ts, DIFF_HEADS * 2 * DIFF_D)

    ig = (col('mi') + w['b_mi']).reshape(B, Ts, MLSTM_HEADS)
    fg = (col('mf') + w['b_mf']).reshape(B, Ts, MLSTM_HEADS)
    ml, mc0, mn0, mm0 = mstate
    y_m, mc, mn, mm = mlstm(ml, proj.reshape(B, Ts, -1), lay, ig, fg, w['g_mout'], mc0, mn0, mm0,
                            math.gcd(Ts, cfg['mlstm_chunk']))

    mix = jnp.concatenate([y_d.astype(BF16), y_m, y_a.astype(BF16)], axis=-1).reshape(T, -1)
    x, h2, logits = out_proj(l, mix, x, g1, wl['w_out'], w['g_norm2'], sc2, sh2, wl['w_router'], w['b_router'],
                             cfg['tm_out'], rows_per_seq)
    state = (k_new, v_new, ckv, kr, mc, mn, mm)
    return x, h2, logits, state


def kernel(x_prompt, x_sample, c_prompt, c_sample, cache_diff_k, cache_diff_v, cache_mla_ckv, cache_mla_kr, state_mlstm_c, state_mlstm_n, state_mlstm_m, page_table, w_ada, b_ada, g_norm1, g_norm2, w_in, g_dq, g_dk, lam_diff, g_dsub, b_mi, b_mf, g_mout, g_aqa, w_aqb, g_akva, w_akvb, g_aqn, g_aqr, g_akn, g_akr, w_out, w_router, b_router, w_gu, b_gu, w_down, b_down):
    Bp, Tp, D = x_prompt.shape
    Bs, Ts, _ = x_sample.shape
    depth = w_in.shape[0]
    n_in_pad = -(-w_in.shape[2] // (5 * LANES)) * (5 * LANES)
    cfg = dict(layout=_in_layout(), tm_in=512, tn_in=n_in_pad // 5, tm_mm=512, tm_out=256,
               t_attn=512, mlstm_chunk=256, tm_moe=256)

    n_c = Bp + Bs
    c_all = jnp.concatenate([c_prompt, c_sample, jnp.zeros((-n_c % SUBLANES, D), F32)], axis=0)
    mod = ada_mod(c_all, w_ada, b_ada)[:, :n_c].reshape(depth, n_c, N_MOD, D)
    kvb = w_akvb.reshape(depth, MLA_KV_RANK, MLA_HEADS, MLA_NOPE + MLA_DV)
    wl = dict(
        w_in=_reorder_w_in(w_in, n_in_pad), w_out=w_out.astype(BF16), w_aqb=w_aqb.astype(BF16),
        w_akvb=w_akvb.astype(BF16), w_router=w_router,
        wkt=jnp.transpose(kvb[..., :MLA_NOPE], (0, 2, 3, 1)).reshape(
            depth, MLA_HEADS * MLA_NOPE, MLA_KV_RANK).astype(BF16),
        wv=kvb[..., MLA_NOPE:].reshape(depth, MLA_KV_RANK, MLA_HEADS * MLA_DV).astype(BF16))
    w_gu_b = w_gu.astype(BF16)
    w_down_b = w_down.astype(BF16)

    n_pool = cache_diff_k.shape[1]
    page = cache_diff_k.shape[2]
    ck = cache_diff_k.reshape(depth, n_pool, page, 2 * DIFF_D)
    cv = cache_diff_v.reshape(depth, n_pool, page, 2 * DIFF_D)
    cr = jnp.swapaxes(cache_mla_kr, 2, 3)
    past = (ck, cv, cache_mla_ckv, cr, page_table)

    pos_p = jnp.arange(Tp, dtype=jnp.int32)
    pos_s = PAST_LEN + jnp.arange(Ts, dtype=jnp.int32)
    zero_state = (0, jnp.zeros((1, Bp, MLSTM_HEADS, MLSTM_DV, MLSTM_DQK), F32),
                  jnp.zeros((1, Bp, MLSTM_HEADS, MLSTM_DQK), F32),
                  jnp.zeros((1, Bp, MLSTM_HEADS), F32))

    xp = x_prompt.reshape(Bp * Tp, D)
    xs = x_sample.reshape(Bs * Ts, D)
    new_p = [[] for _ in range(7)]
    new_s = [[] for _ in range(7)]
    for l in range(depth):
        w = dict(g_norm1=g_norm1[l], g_norm2=g_norm2[l], g_dq=g_dq[l], g_dk=g_dk[l],
                 g_dsub=g_dsub[l], b_mi=b_mi[l], b_mf=b_mf[l], g_mout=g_mout[l], g_aqa=g_aqa[l],
                 g_akva=g_akva[l], g_aqn=g_aqn[l], g_aqr=g_aqr[l], g_akn=g_akn[l], g_akr=g_akr[l],
                 b_router=b_router[l])
        lam_init = 0.8 - 0.6 * math.exp(-0.3 * l)
        lv = lam_diff[l]
        lam = jnp.exp(jnp.sum(lv[0] * lv[1])) - jnp.exp(jnp.sum(lv[2] * lv[3])) + lam_init
        mods_p = tuple(mod[l, :Bp, i][:, None, :] for i in range(N_MOD))
        mods_s = tuple(jnp.repeat(mod[l, Bp:, i], Ts, axis=0) for i in range(N_MOD))

        xp, h2p, lgp, st_p = _layer(l, xp, mods_p, Tp, (Bp, Tp), pos_p, w, wl, lam, lam_init, None, zero_state, cfg)
        xs, h2s, lgs, st_s = _layer(l, xs, mods_s, Ts, (Bs, Ts), pos_s, w, wl, lam, lam_init, past,
                                    (l, state_mlstm_c, state_mlstm_n, state_mlstm_m), cfg)

        y = moe(l, jnp.concatenate([h2p, h2s], axis=0), jnp.concatenate([lgp, lgs], axis=0),
                w_gu_b, b_gu, w_down_b, b_down, cfg['tm_moe'])
        xp = (xp.reshape(Bp, Tp, D) + mods_p[5] * y[:Bp * Tp].reshape(Bp, Tp, D)).reshape(Bp * Tp, D)
        xs = xs + mods_s[5] * y[Bp * Tp:]
        for lst, a in zip(new_p, st_p):
            lst.append(a)
        for lst, a in zip(new_s, st_s):
            lst.append(a)

    outs_p = [jnp.stack(a) for a in new_p]
    outs_s = [jnp.stack(a) for a in new_s]
    return (xp.reshape(Bp, Tp, D), xs.reshape(Bs, Ts, D), *outs_p, *outs_s)
```

```python
import functools
import math

import jax
import jax.numpy as jnp
from jax import lax
from jax.experimental import pallas as pl
from jax.experimental.pallas import tpu as pltpu

F32 = jnp.float32
BF16 = jnp.bfloat16

PAST_LEN = 8192
DIFF_HEADS = 4
DIFF_D = 64
MLSTM_HEADS = 4
MLSTM_DQK = 128
MLSTM_DV = 256
MLA_HEADS = 4
MLA_Q_RANK = 384
MLA_KV_RANK = 128
MLA_NOPE = 128
MLA_ROPE = 32
MLA_DV = 128
N_EXPERTS = 32
TOP_K = 4
SWIGLU_LIMIT = 7.0
SWIGLU_ALPHA = 1.702
ROPE_THETA = 10000.0
EPS = 1e-6
N_MOD = 6
DIFF_SCALE = DIFF_D ** -0.5
MLA_SCALE = (MLA_NOPE + MLA_ROPE) ** -0.5

LANES = 128
SUBLANES = 8
VMEM_LIMIT_BYTES = 56 * 1024 * 1024
NEG_BIG = -1e30

_NT = (((1,), (1,)), ((), ()))
_TN = (((0,), (0,)), ((), ()))


def _params(semantics, vmem=VMEM_LIMIT_BYTES):
    return pltpu.CompilerParams(dimension_semantics=semantics, vmem_limit_bytes=vmem)


def _ada_kernel(c_ref, w_ref, b_ref, o_ref):
    c = c_ref[...]
    a = (c * jax.nn.sigmoid(c)).astype(BF16)
    o_ref[...] = jnp.dot(a, w_ref[...].astype(BF16), preferred_element_type=F32) + b_ref[...]


def ada_mod(c, w_ada, b_ada, tn=1024):
    L, D, N = w_ada.shape
    Bc = c.shape[0]
    tn = math.gcd(N, tn)
    return pl.pallas_call(
        _ada_kernel,
        grid=(L, N // tn),
        in_specs=[pl.BlockSpec((Bc, D), lambda l, j: (0, 0)),
                  pl.BlockSpec((None, D, tn), lambda l, j: (l, 0, j)),
                  pl.BlockSpec((None, 1, tn), lambda l, j: (l, 0, j))],
        out_specs=pl.BlockSpec((None, Bc, tn), lambda l, j: (l, 0, j)),
        out_shape=jax.ShapeDtypeStruct((L, Bc, N), F32),
        compiler_params=_params(("arbitrary", "arbitrary")),
        name="ada_mod",
    )(c, w_ada, b_ada.reshape(L, 1, N))


def _matmul_kernel(x_ref, w_ref, o_ref):
    o_ref[...] = jnp.dot(x_ref[...].astype(BF16), w_ref[...].astype(BF16),
                         preferred_element_type=F32).astype(o_ref.dtype)


def matmul(layer, x, w, tm, out_dtype=F32):
    M, K = x.shape
    N = w.shape[2]
    tm = min(tm, M)
    return pl.pallas_call(
        _matmul_kernel,
        grid=(M // tm,),
        in_specs=[pl.BlockSpec((tm, K), lambda i: (i, 0)),
                  pl.BlockSpec((None, K, N), lambda i: (layer, 0, 0))],
        out_specs=pl.BlockSpec((tm, N), lambda i: (i, 0)),
        out_shape=jax.ShapeDtypeStruct((M, N), out_dtype),
        compiler_params=_params(("arbitrary",)),
        name="matmul",
    )(x, w)


def _rms(x):
    return x * lax.rsqrt(jnp.mean(x * x, axis=-1, keepdims=True) + EPS)


def _inproj_kernel(x_ref, g_ref, sc_ref, sh_ref, w_ref, o_ref, h_scr):
    @pl.when(pl.program_id(1) == 0)
    def _():
        y = _rms(x_ref[...]) * g_ref[...]
        h_scr[...] = (y * (1.0 + sc_ref[...]) + sh_ref[...]).astype(BF16)

    o_ref[...] = jnp.dot(h_scr[...], w_ref[...], preferred_element_type=F32)


def _mod_spec(mod, tm, rows_per_seq):
    if mod.ndim == 3:
        tiles = rows_per_seq // tm
        return pl.BlockSpec((None, 1, mod.shape[-1]), lambda i, *_: (i // tiles, 0, 0))
    return pl.BlockSpec((tm, mod.shape[-1]), lambda i, *_: (i, 0))


def in_proj(layer, x, g, sc, sh, w, tm, tn, rows_per_seq):
    T, D = x.shape
    N = w.shape[2]
    tm = min(tm, T)
    return pl.pallas_call(
        _inproj_kernel,
        grid=(T // tm, N // tn),
        in_specs=[pl.BlockSpec((tm, D), lambda i, j: (i, 0)),
                  pl.BlockSpec((1, D), lambda i, j: (0, 0)),
                  _mod_spec(sc, tm, rows_per_seq),
                  _mod_spec(sh, tm, rows_per_seq),
                  pl.BlockSpec((None, D, tn), lambda i, j: (layer, 0, j))],
        out_specs=pl.BlockSpec((tm, tn), lambda i, j: (i, j)),
        out_shape=jax.ShapeDtypeStruct((T, N), F32),
        scratch_shapes=[pltpu.VMEM((tm, D), BF16)],
        compiler_params=_params(("arbitrary", "arbitrary")),
        name="in_proj",
    )(x, g.reshape(1, D), sc, sh, w)


def _outproj_kernel(mix_ref, x_ref, g1_ref, w_ref, gn_ref, sc_ref, sh_ref, wr_ref, br_ref,
                    xo_ref, h2_ref, lg_ref):
    x = x_ref[...] + g1_ref[...] * jnp.dot(mix_ref[...], w_ref[...], preferred_element_type=F32)
    xo_ref[...] = x
    h2 = (_rms(x) * gn_ref[...]) * (1.0 + sc_ref[...]) + sh_ref[...]
    h2_ref[...] = h2
    lg_ref[...] = jnp.dot(h2, wr_ref[...], precision=lax.Precision.HIGHEST,
                          preferred_element_type=F32) + br_ref[...]


def out_proj(layer, mix, x, g1, w, gn, sc, sh, wr, br, tm, rows_per_seq):
    T, D = x.shape
    K = mix.shape[1]
    E = wr.shape[2]
    tm = min(tm, T)
    return pl.pallas_call(
        _outproj_kernel,
        grid=(T // tm,),
        in_specs=[pl.BlockSpec((tm, K), lambda i: (i, 0)),
                  pl.BlockSpec((tm, D), lambda i: (i, 0)),
                  _mod_spec(g1, tm, rows_per_seq),
                  pl.BlockSpec((None, K, D), lambda i: (layer, 0, 0)),
                  pl.BlockSpec((1, D), lambda i: (0, 0)),
                  _mod_spec(sc, tm, rows_per_seq),
                  _mod_spec(sh, tm, rows_per_seq),
                  pl.BlockSpec((None, D, E), lambda i: (layer, 0, 0)),
                  pl.BlockSpec((1, E), lambda i: (0, 0))],
        out_specs=[pl.BlockSpec((tm, D), lambda i: (i, 0)),
                   pl.BlockSpec((tm, D), lambda i: (i, 0)),
                   pl.BlockSpec((tm, E), lambda i: (i, 0))],
        out_shape=[jax.ShapeDtypeStruct((T, D), F32),
                   jax.ShapeDtypeStruct((T, D), F32),
                   jax.ShapeDtypeStruct((T, E), F32)],
        compiler_params=_params(("arbitrary",)),
        name="out_proj",
    )(mix, x, g1, w, gn.reshape(1, D), sc, sh, wr, br.reshape(1, E))


def _flash_kernel(*refs, has_rope, t):
    if has_rope:
        q_ref, k_ref, v_ref, qr_ref, kr_ref, o_ref = refs
        qr = qr_ref[...]
    else:
        q_ref, k_ref, v_ref, o_ref = refs
    i = pl.program_id(2)
    q = q_ref[...]
    dv = v_ref.shape[-1]

    def block(off, masked, carry):
        m_prev, l_prev, acc = carry
        s = lax.dot_general(q, k_ref[pl.ds(off, t), :], _NT, preferred_element_type=F32)
        if has_rope:
            s = s + lax.dot_general(qr, kr_ref[pl.ds(off, t), :], _NT, preferred_element_type=F32)
        if masked:
            row = lax.broadcasted_iota(jnp.int32, s.shape, 0)
            col = lax.broadcasted_iota(jnp.int32, s.shape, 1)
            s = jnp.where(col <= row, s, NEG_BIG)
        m_new = jnp.maximum(m_prev, jnp.max(s, axis=1, keepdims=True))
        alpha = jnp.exp(m_prev - m_new)
        p = jnp.exp(s - m_new)
        return (m_new, alpha * l_prev + jnp.sum(p, axis=1, keepdims=True),
                alpha * acc + jnp.dot(p.astype(BF16), v_ref[pl.ds(off, t), :], preferred_element_type=F32))

    init = (jnp.full((t, 1), NEG_BIG, F32), jnp.zeros((t, 1), F32), jnp.zeros((t, dv), F32))
    carry = lax.fori_loop(0, i, lambda j, c: block(pl.multiple_of(j * t, t), False, c), init)
    _, l, acc = block(pl.multiple_of(i * t, t), True, carry)
    o_ref[...] = acc / l


def flash_attention(q, k, v, kmap, vmap, t, qr=None, kr=None):
    B, H, T, dk = q.shape
    dv = v.shape[-1]
    in_specs = [pl.BlockSpec((None, None, t, dk), lambda b, h, i: (b, h, i, 0)),
                pl.BlockSpec((None, None, T, dk), lambda b, h, i: (b, kmap(h), 0, 0)),
                pl.BlockSpec((None, None, T, dv), lambda b, h, i: (b, vmap(h), 0, 0))]
    args = [q, k, v]
    if qr is not None:
        dr = qr.shape[-1]
        in_specs += [pl.BlockSpec((None, None, t, dr), lambda b, h, i: (b, h, i, 0)),
                     pl.BlockSpec((None, None, T, dr), lambda b, h, i: (b, 0, 0, 0))]
        args += [qr, kr]
    return pl.pallas_call(
        functools.partial(_flash_kernel, has_rope=qr is not None, t=t),
        grid=(B, H, T // t),
        in_specs=in_specs,
        out_specs=pl.BlockSpec((None, None, t, dv), lambda b, h, i: (b, h, i, 0)),
        out_shape=jax.ShapeDtypeStruct((B, H, T, dv), F32),
        compiler_params=_params(("arbitrary",) * 3),
        name="flash_attention",
    )(*args)


N_PAGED = 4


def _sample_attn_kernel(pt_ref, lam_ref, qd_ref, knew_ref, vnew_ref, qn_ref, qr_ref, wkt_ref, wv_ref,
                        cnew_ref, rnew_ref, kc_hbm, vc_hbm, cc_hbm, rc_hbm, od_ref, oa_ref,
                        kbuf, vbuf, cbuf, rbuf, sem, s_scr, *, layer, n_pages, page, chunk, ts):
    b = pl.program_id(0)
    nb = pl.num_programs(0)
    slot = b % 2
    tp = n_pages * page
    n_chunks = tp // chunk
    hd = MLA_HEADS
    rd = 2 * DIFF_HEADS * ts
    ra = MLA_HEADS * ts
    hbm = (kc_hbm, vc_hbm, cc_hbm, rc_hbm)
    bufs = (kbuf, vbuf, cbuf, rbuf)

    def tokens(buf, sl, start, size):
        if buf is rbuf:
            return buf.at[sl, :, pl.ds(start, size)]
        return buf.at[sl, pl.ds(start, size)]

    def start_gather(seq, sl):
        def body(p, carry):
            pg = pt_ref[seq * n_pages + p]
            for a in range(N_PAGED):
                pltpu.make_async_copy(hbm[a].at[layer, pg],
                                      tokens(bufs[a], sl, pl.multiple_of(p * page, page), page),
                                      sem.at[a, sl]).start()
            return carry
        lax.fori_loop(0, n_pages, body, 0)

    def wait_gather(sl):
        for a in range(N_PAGED):
            past = tokens(bufs[a], sl, 0, tp)
            pltpu.make_async_copy(past, past, sem.at[a, sl]).wait()

    @pl.when(b == 0)
    def _():
        for sl in range(2):
            for buf in (kbuf, vbuf, cbuf):
                buf[sl, tp:tp + page, :] = jnp.zeros((page, buf.shape[-1]), F32)
            rbuf[sl, :, tp:tp + page] = jnp.zeros((rbuf.shape[1], page), F32)
        start_gather(0, 0)

    @pl.when(b + 1 < nb)
    def _():
        start_gather(b + 1, 1 - slot)

    kbuf[slot, tp:tp + ts, :] = knew_ref[...]
    vbuf[slot, tp:tp + ts, :] = vnew_ref[...]
    cbuf[slot, tp:tp + ts, :] = cnew_ref[...]
    rbuf[slot, :, tp:tp + ts] = rnew_ref[...]

    qd = qd_ref[...]
    qr = qr_ref[...]
    wkt = wkt_ref[...]
    qabs = jnp.dot(qn_ref[...], wkt, preferred_element_type=F32).astype(BF16)
    lhs = jnp.concatenate([wkt, qabs], axis=0)

    def scores(off, n):
        kc = kbuf[slot, pl.ds(off, n), :].astype(BF16)
        cc = cbuf[slot, pl.ds(off, n), :].astype(BF16)
        rc = rbuf[slot, :, pl.ds(off, n)].astype(BF16)
        sd = lax.dot_general(qd, kc, _NT, preferred_element_type=F32)
        big = lax.dot_general(lhs, cc, _NT, preferred_element_type=F32)
        sr = jnp.dot(qr, rc, preferred_element_type=F32)
        parts = [sd]
        for h in range(hd):
            kn = big[h * MLA_NOPE:(h + 1) * MLA_NOPE, :]
            ss = jnp.sum(kn * kn, axis=0, keepdims=True)
            rinv = lax.rsqrt(ss * (1.0 / MLA_NOPE) + EPS)
            lo = hd * MLA_NOPE + h * ts
            parts.append(big[lo:lo + ts, :] * rinv + sr[h * ts:(h + 1) * ts, :])
        return jnp.concatenate(parts, axis=0)

    def weighted(p, off, n):
        pb = p.astype(BF16)
        vc = vbuf[slot, pl.ds(off, n), :].astype(BF16)
        cc = cbuf[slot, pl.ds(off, n), :].astype(BF16)
        return (jnp.dot(pb[:rd], vc, preferred_element_type=F32),
                jnp.dot(pb[rd:], cc, preferred_element_type=F32))

    s_tail = scores(tp, page)
    row = lax.broadcasted_iota(jnp.int32, s_tail.shape, 0)
    col = lax.broadcasted_iota(jnp.int32, s_tail.shape, 1)
    s_tail = jnp.where(col <= row % ts, s_tail, NEG_BIG)
    m0 = jnp.max(s_tail, axis=1, keepdims=True)

    wait_gather(slot)

    def pass1(c, m):
        s = scores(pl.multiple_of(c * chunk, chunk), chunk)
        s_scr[c] = s
        return jnp.maximum(m, jnp.max(s, axis=1, keepdims=True))

    m = lax.fori_loop(0, n_chunks, pass1, m0)

    p_tail = jnp.exp(s_tail - m)
    l0 = jnp.sum(p_tail, axis=1, keepdims=True)
    accd0, acca0 = weighted(p_tail, tp, page)

    def pass2(c, carry):
        l, accd, acca = carry
        p = jnp.exp(s_scr[c] - m)
        dd, da = weighted(p, pl.multiple_of(c * chunk, chunk), chunk)
        return l + jnp.sum(p, axis=1, keepdims=True), accd + dd, acca + da

    l, accd, acca = lax.fori_loop(0, n_chunks, pass2, (l0, accd0, acca0))

    on = accd / l[:rd]
    half = rd // 2
    od_ref[...] = on[:half] - lam_ref[0, 0] * on[half:]
    lat = (acca / l[rd:]).astype(BF16)
    full = jnp.dot(lat, wv_ref[...], preferred_element_type=F32)
    oa_ref[...] = jnp.concatenate(
        [full[h * ts:(h + 1) * ts, h * MLA_DV:(h + 1) * MLA_DV] for h in range(hd)], axis=0)


def sample_attention(layer, page_table, lam, qd, knew, vnew, qn, qr, wkt, wv, cnew, rnew,
                     cache_k, cache_v, cache_c, cache_r, chunk=1024):
    DB, n_pages = page_table.shape
    page = cache_k.shape[2]
    ts = knew.shape[1]
    rd = 2 * DIFF_HEADS * ts
    ra = MLA_HEADS * ts
    tp = n_pages * page
    chunk = min(chunk, tp)
    kern = functools.partial(_sample_attn_kernel, layer=layer, n_pages=n_pages, page=page, chunk=chunk, ts=ts)
    seq3 = lambda b, pt: (b, 0, 0)
    wl = lambda b, pt: (layer, 0, 0)
    any_spec = pl.BlockSpec(memory_space=pl.ANY)
    grid_spec = pltpu.PrefetchScalarGridSpec(
        num_scalar_prefetch=1,
        grid=(DB,),
        in_specs=[pl.BlockSpec(memory_space=pltpu.SMEM),
                  pl.BlockSpec((None, rd, 2 * DIFF_D), seq3),
                  pl.BlockSpec((None, ts, 2 * DIFF_D), seq3),
                  pl.BlockSpec((None, ts, 2 * DIFF_D), seq3),
                  pl.BlockSpec((None, ra, MLA_HEADS * MLA_NOPE), seq3),
                  pl.BlockSpec((None, ra, MLA_ROPE), seq3),
                  pl.BlockSpec((None, MLA_HEADS * MLA_NOPE, MLA_KV_RANK), wl),
                  pl.BlockSpec((None, MLA_KV_RANK, MLA_HEADS * MLA_DV), wl),
                  pl.BlockSpec((None, ts, MLA_KV_RANK), seq3),
                  pl.BlockSpec((None, MLA_ROPE, ts), seq3),
                  any_spec, any_spec, any_spec, any_spec],
        out_specs=[pl.BlockSpec((None, rd // 2, 2 * DIFF_D), seq3),
                   pl.BlockSpec((None, ra, MLA_DV), seq3)],
        scratch_shapes=[pltpu.VMEM((2, tp + page, 2 * DIFF_D), F32),
                        pltpu.VMEM((2, tp + page, 2 * DIFF_D), F32),
                        pltpu.VMEM((2, tp + page, MLA_KV_RANK), F32),
                        pltpu.VMEM((2, MLA_ROPE, tp + page), F32),
                        pltpu.SemaphoreType.DMA((N_PAGED, 2)),
                        pltpu.VMEM((tp // chunk, rd + ra, chunk), F32)],
    )
    return pl.pallas_call(
        kern,
        grid_spec=grid_spec,
        out_shape=[jax.ShapeDtypeStruct((DB, rd // 2, 2 * DIFF_D), F32),
                   jax.ShapeDtypeStruct((DB, ra, MLA_DV), F32)],
        compiler_params=_params(("arbitrary",)),
        name="sample_attention",
    )(page_table.reshape(-1), lam, qd, knew, vnew, qn, qr, wkt, wv, cnew, rnew,
      cache_k, cache_v, cache_c, cache_r)


def _log_sigmoid(x):
    return jnp.minimum(x, 0.0) - jnp.log(1.0 + jnp.exp(-jnp.abs(x)))


def _mlstm_kernel(q_ref, k_ref, v_ref, o_ref, icol_ref, irow_ref, fcol_ref, frow_ref, g_ref,
                  c0_ref, n0_ref, m0_ref, y_ref, c_ref, n_ref, m_ref, *, L):
    ci = pl.program_id(1)
    DQK, DV = MLSTM_DQK, MLSTM_DV

    @pl.when(ci == 0)
    def _():
        c_ref[...] = c0_ref[...]
        n_ref[...] = n0_ref[...]
        m_ref[...] = m0_ref[...]

    t_idx = lax.broadcasted_iota(jnp.int32, (L, L), 0)
    s_idx = lax.broadcasted_iota(jnp.int32, (L, L), 1)
    causal = s_idx <= t_idx
    k_scale = DQK ** -0.5

    for h in range(MLSTM_HEADS):
        q = q_ref[:, h * DQK:(h + 1) * DQK].astype(BF16)
        k = k_ref[:, h * DQK:(h + 1) * DQK].astype(BF16)
        v = v_ref[:, h * DV:(h + 1) * DV].astype(BF16)
        i_col = icol_ref[h]
        i_row = irow_ref[h]
        f_col = _log_sigmoid(fcol_ref[h])
        f_row = _log_sigmoid(frow_ref[h])
        C = c_ref[h]
        n = n_ref[h]
        m_prev = m_ref[h]

        b_col = jnp.sum(jnp.where(causal, f_row, 0.0), axis=1, keepdims=True)
        b_row = jnp.sum(jnp.where(t_idx <= s_idx, f_col, 0.0), axis=0, keepdims=True)
        b_end = b_col[L - 1:L, :]

        logw = jnp.where(causal, b_col - b_row + i_row, -jnp.inf)
        inter = b_col + m_prev
        m_t = jnp.maximum(inter, jnp.max(logw, axis=1, keepdims=True))
        qk = lax.dot_general(q, k, _NT, preferred_element_type=F32) * k_scale
        s = jnp.exp(logw - m_t) * qk
        a = jnp.exp(inter - m_t)
        num = jnp.dot(s.astype(BF16), v, preferred_element_type=F32) \
            + a * lax.dot_general(q, C.astype(BF16), _NT, preferred_element_type=F32)
        den = jnp.sum(s, axis=1, keepdims=True) + a * jnp.sum(q.astype(F32) * n, axis=1, keepdims=True)
        hh = num / jnp.maximum(jnp.abs(den), jnp.exp(-m_t))

        m_new = m_t[L - 1:L, :]
        w_end = jnp.exp(b_end - b_col + i_col - m_new)
        decay = jnp.exp(b_end + m_prev - m_new)
        wv = (w_end * v.astype(F32)).astype(BF16)
        c_ref[h] = decay * C + k_scale * lax.dot_general(wv, k, _TN, preferred_element_type=F32)
        n_ref[h] = decay * n + k_scale * jnp.sum(w_end * k.astype(F32), axis=0, keepdims=True)
        m_ref[h] = m_new

        hn = _rms(hh) * g_ref[:, h * DV:(h + 1) * DV]
        y_ref[:, h * DV:(h + 1) * DV] = (jax.nn.sigmoid(o_ref[:, h * DV:(h + 1) * DV]) * hn).astype(y_ref.dtype)


def mlstm(layer, proj, lay, ig, fg, g, c0, n0, m0, L):
    B, T, _ = proj.shape
    H, DQK, DV = MLSTM_HEADS, MLSTM_DQK, MLSTM_DV
    Ls = c0.shape[0]
    nc = T // L

    def cols(name, width):
        start, end = lay[name]
        assert end - start == width and start % width == 0
        return pl.BlockSpec((None, L, width), lambda b, c: (b, c, start // width))

    icol = jnp.transpose(ig, (0, 2, 1))[..., None]
    irow = jnp.transpose(ig, (0, 2, 1))[:, :, None, :]
    fcol = jnp.transpose(fg, (0, 2, 1))[..., None]
    frow = jnp.transpose(fg, (0, 2, 1))[:, :, None, :]
    tok = lambda b, c: (b, c, 0)
    col = lambda b, c: (b, 0, c, 0)
    rowm = lambda b, c: (b, 0, 0, c)
    st0 = lambda b, c: (layer, b, 0, 0, 0)
    st = lambda b, c: (b, 0, 0, 0)
    y, c, n, m = pl.pallas_call(
        functools.partial(_mlstm_kernel, L=L),
        grid=(B, nc),
        in_specs=[cols('mq', H * DQK), cols('mk', H * DQK), cols('mv', H * DV), cols('mo', H * DV),
                  pl.BlockSpec((None, H, L, 1), col),
                  pl.BlockSpec((None, H, 1, L), rowm),
                  pl.BlockSpec((None, H, L, 1), col),
                  pl.BlockSpec((None, H, 1, L), rowm),
                  pl.BlockSpec((1, H * DV), lambda b, c: (0, 0)),
                  pl.BlockSpec((None, None, H, DV, DQK), st0),
                  pl.BlockSpec((None, None, H, 1, DQK), st0),
                  pl.BlockSpec((None, None, H, 1, 1), st0)],
        out_specs=[pl.BlockSpec((None, L, H * DV), tok),
                   pl.BlockSpec((None, H, DV, DQK), st),
                   pl.BlockSpec((None, H, 1, DQK), st),
                   pl.BlockSpec((None, H, 1, 1), st)],
        out_shape=[jax.ShapeDtypeStruct((B, T, H * DV), BF16),
                   jax.ShapeDtypeStruct((B, H, DV, DQK), F32),
                   jax.ShapeDtypeStruct((B, H, 1, DQK), F32),
                   jax.ShapeDtypeStruct((B, H, 1, 1), F32)],
        compiler_params=_params(("arbitrary",) * 2),
        name="mlstm",
    )(proj, proj, proj, proj, icol, irow, fcol, frow, g.reshape(1, H * DV),
      c0, n0.reshape(Ls, B, H, 1, DQK), m0.reshape(Ls, B, H, 1, 1))
    return y, c, n.reshape(B, H, DQK), m.reshape(B, H)


def _refresh_expert_weight(be_ref, w_ref, w_bf):
    blk = pl.program_id(0)

    @pl.when((blk == 0) | (be_ref[blk] != be_ref[jnp.maximum(blk - 1, 0)]))
    def _():
        w_bf[...] = w_ref[...].astype(BF16)


def _moe_up_kernel(be_ref, nu_ref, x_ref, wgu_ref, bgu_ref, act_ref, w_bf, *, f):
    blk = pl.program_id(0)
    _refresh_expert_weight(be_ref, wgu_ref, w_bf)

    @pl.when(blk < nu_ref[0])
    def _():
        gu = jnp.dot(x_ref[...].astype(BF16), w_bf[...], preferred_element_type=F32) + bgu_ref[...]
        g_ = jnp.minimum(gu[:, :f], SWIGLU_LIMIT)
        u_ = jnp.clip(gu[:, f:], -SWIGLU_LIMIT, SWIGLU_LIMIT)
        act_ref[...] = (g_ * jax.nn.sigmoid(SWIGLU_ALPHA * g_) * (u_ + 1.0)).astype(BF16)

    @pl.when(blk >= nu_ref[0])
    def _():
        act_ref[...] = jnp.zeros(act_ref.shape, BF16)


def _moe_down_kernel(be_ref, nu_ref, act_ref, gate_ref, wd_ref, bd_ref, o_ref, w_bf):
    blk = pl.program_id(0)
    _refresh_expert_weight(be_ref, wd_ref, w_bf)

    @pl.when(blk < nu_ref[0])
    def _():
        y = jnp.dot(act_ref[...], w_bf[...], preferred_element_type=F32) + bd_ref[...]
        o_ref[...] = y * gate_ref[...]

    @pl.when(blk >= nu_ref[0])
    def _():
        o_ref[...] = jnp.zeros(o_ref.shape, F32)


def moe_experts(layer, blk_e, n_used, xb, row_gate, w_gu, b_gu, w_down, b_down, tm):
    R, D = xb.shape
    L, E, _, F2 = w_gu.shape
    f = F2 // 2
    wmap = lambda i, be, nu: (layer, be[i], 0, 0)
    rows = lambda i, be, nu: (i, 0)
    act = pl.pallas_call(
        functools.partial(_moe_up_kernel, f=f),
        grid_spec=pltpu.PrefetchScalarGridSpec(
            num_scalar_prefetch=2,
            grid=(R // tm,),
            in_specs=[pl.BlockSpec((tm, D), rows),
                      pl.BlockSpec((None, None, D, F2), wmap),
                      pl.BlockSpec((None, None, 1, F2), wmap)],
            out_specs=pl.BlockSpec((tm, f), rows),
            scratch_shapes=[pltpu.VMEM((D, F2), BF16)]),
        out_shape=jax.ShapeDtypeStruct((R, f), BF16),
        compiler_params=_params(("arbitrary",)),
        name="moe_up",
    )(blk_e, n_used, xb, w_gu, b_gu.reshape(L, E, 1, F2))
    return pl.pallas_call(
        _moe_down_kernel,
        grid_spec=pltpu.PrefetchScalarGridSpec(
            num_scalar_prefetch=2,
            grid=(R // tm,),
            in_specs=[pl.BlockSpec((tm, f), rows),
                      pl.BlockSpec((tm, 1), rows),
                      pl.BlockSpec((None, None, f, D), wmap),
                      pl.BlockSpec((None, None, 1, D), wmap)],
            out_specs=pl.BlockSpec((tm, D), rows),
            scratch_shapes=[pltpu.VMEM((f, D), BF16)]),
        out_shape=jax.ShapeDtypeStruct((R, D), F32),
        compiler_params=_params(("arbitrary",)),
        name="moe_down",
    )(blk_e, n_used, act, row_gate.reshape(R, 1), w_down, b_down.reshape(L, E, 1, D))


def moe(layer, h2, logits, w_gu, b_gu, w_down, b_down, tm):
    T, D = h2.shape
    E = N_EXPERTS
    A = T * TOP_K
    top_v, top_e = lax.top_k(logits, TOP_K)
    gate = jax.nn.softmax(top_v, axis=-1)
    flat_e = top_e.reshape(-1)
    flat_g = gate.reshape(-1)
    onehot = flat_e[:, None] == jnp.arange(E, dtype=flat_e.dtype)[None, :]
    running = jnp.cumsum(onehot.astype(jnp.int32), axis=0)
    counts = running[-1]
    rank = jnp.sum(jnp.where(onehot, running, 0), axis=1) - 1
    pcounts = (counts + tm - 1) // tm * tm
    starts = jnp.cumsum(counts) - counts
    pends = jnp.cumsum(pcounts)
    pstarts = pends - pcounts
    pos = jnp.sum(jnp.where(onehot, pstarts[None, :], 0), axis=1) + rank
    order = jnp.argsort(flat_e)
    n_rows = (-(-A // tm) + E) * tm
    n_blocks = n_rows // tm
    blk_start = jnp.arange(n_blocks, dtype=jnp.int32) * tm
    blk_e = jnp.minimum(jnp.sum(pends[None, :] <= blk_start[:, None], axis=1), E - 1).astype(jnp.int32)
    n_used = (pends[-1] // tm).astype(jnp.int32).reshape(1)
    off = ((blk_start - pstarts[blk_e])[:, None] + jnp.arange(tm, dtype=jnp.int32)[None, :])
    valid = (off < counts[blk_e][:, None]).reshape(-1)
    src = order[jnp.clip(starts[blk_e][:, None] + off, 0, A - 1).reshape(-1)]
    row_tok = jnp.where(valid, src // TOP_K, T).astype(jnp.int32)
    row_gate = jnp.where(valid, flat_g[src], 0.0)
    h_pad = jnp.concatenate([h2, jnp.zeros((1, D), h2.dtype)], axis=0)
    xb = h_pad[row_tok]
    yb = moe_experts(layer, blk_e, n_used, xb, row_gate, w_gu, b_gu, w_down, b_down, tm)
    pos_km = jnp.transpose(pos.reshape(T, TOP_K)).reshape(-1)
    return jnp.sum(yb[pos_km].reshape(TOP_K, T, D), axis=0)


def _gnorm(x, g):
    return x * lax.rsqrt(jnp.mean(x * x, axis=-1, keepdims=True) + EPS) * g


def _rope_tables(pos, half):
    inv_freq = ROPE_THETA ** (-jnp.arange(half, dtype=F32) / half)
    ang = pos.astype(F32)[:, None] * inv_freq[None, :]
    return jnp.cos(ang), jnp.sin(ang)


def _rope(x, cos, sin):
    half = x.shape[-1] // 2
    shp = (cos.shape[0],) + (1,) * (x.ndim - 3) + (half,)
    c = cos.reshape(shp)
    s = sin.reshape(shp)
    x1, x2 = x[..., :half], x[..., half:]
    return jnp.concatenate([x1 * c - x2 * s, x2 * c + x1 * s], axis=-1)


_W_IN_GIVEN = [('dq', DIFF_HEADS * 2 * DIFF_D), ('dk', 2 * DIFF_D), ('dv', 2 * DIFF_D),
               ('mq', MLSTM_HEADS * MLSTM_DQK), ('mk', MLSTM_HEADS * MLSTM_DQK),
               ('mv', MLSTM_HEADS * MLSTM_DV), ('mo', MLSTM_HEADS * MLSTM_DV),
               ('mi', MLSTM_HEADS), ('mf', MLSTM_HEADS),
               ('aq', MLA_Q_RANK), ('akv', MLA_KV_RANK), ('ar', MLA_ROPE)]
_W_IN_ORDER = ['mv', 'mo', 'mq', 'mk', 'dq', 'dk', 'dv', 'aq', 'akv', 'ar', 'mi', 'mf']


def _offsets(names, sizes):
    out, off = {}, 0
    for name in names:
        out[name] = (off, off + sizes[name])
        off += sizes[name]
    return out


def _in_layout():
    return _offsets(_W_IN_ORDER, dict(_W_IN_GIVEN))


def _reorder_w_in(w_in, n_pad):
    given = _offsets([n for n, _ in _W_IN_GIVEN], dict(_W_IN_GIVEN))
    parts = [w_in[..., given[n][0]:given[n][1]] for n in _W_IN_ORDER]
    parts.append(jnp.zeros(w_in.shape[:-1] + (n_pad - w_in.shape[-1],), w_in.dtype))
    return jnp.concatenate(parts, axis=-1).astype(BF16)


def _layer(l, x, mods, rows_per_seq, seq_shape, pos, w, wl, lam, lam_init, past, mstate, cfg):
    B, Ts = seq_shape
    T, D = x.shape
    sh1, sc1, g1, sh2, sc2, g2 = mods
    lay = cfg['layout']
    proj = in_proj(l, x, w['g_norm1'], sc1, sh1, wl['w_in'], cfg['tm_in'], cfg['tn_in'], rows_per_seq)
    col = lambda name: proj[:, lay[name][0]:lay[name][1]]
    cos64, sin64 = _rope_tables(pos, DIFF_D // 2)
    cos32, sin32 = _rope_tables(pos, MLA_ROPE // 2)

    dq = _rope(_gnorm(col('dq').reshape(B, Ts, DIFF_HEADS, 2, DIFF_D), w['g_dq']), cos64, sin64)
    dk = _rope(_gnorm(col('dk').reshape(B, Ts, 1, 2, DIFF_D), w['g_dk']), cos64, sin64)
    k_new = dk.reshape(B, Ts, 1, 2 * DIFF_D)
    v_new = col('dv').reshape(B, Ts, 1, 2 * DIFF_D)

    aqn = _gnorm(col('aq'), w['g_aqa'])
    qfull = matmul(l, aqn, wl['w_aqb'], cfg['tm_mm']).reshape(B, Ts, MLA_HEADS, MLA_NOPE + MLA_ROPE)
    q_nope = _gnorm(qfull[..., :MLA_NOPE], w['g_aqn'])
    q_rope = _rope(_gnorm(qfull[..., MLA_NOPE:], w['g_aqr']), cos32, sin32)
    ckv = _gnorm(col('akv'), w['g_akva']).reshape(B, Ts, MLA_KV_RANK)
    kr = _rope(_gnorm(col('ar'), w['g_akr']).reshape(B, Ts, MLA_ROPE), cos32, sin32)

    if past is None:
        qd = jnp.transpose(dq * DIFF_SCALE, (0, 3, 2, 1, 4)).reshape(B, 2 * DIFF_HEADS, Ts, DIFF_D).astype(BF16)
        kd = jnp.transpose(dk[:, :, 0], (0, 2, 1, 3)).astype(BF16)
        vd = jnp.transpose(v_new, (0, 2, 1, 3)).astype(BF16)
        od = flash_attention(qd, kd, vd, lambda h: h // DIFF_HEADS, lambda h: 0, cfg['t_attn'])
        o1 = od[:, :DIFF_HEADS]
        o2 = od[:, DIFF_HEADS:]
        o_diff = jnp.transpose(o1 - lam * o2, (0, 2, 1, 3))

        kv = matmul(l, ckv.reshape(T, MLA_KV_RANK), wl['w_akvb'], cfg['tm_mm']).reshape(
            B, Ts, MLA_HEADS, MLA_NOPE + MLA_DV)
        k_nope = _gnorm(kv[..., :MLA_NOPE], w['g_akn'])
        qa = jnp.transpose(q_nope * MLA_SCALE, (0, 2, 1, 3)).astype(BF16)
        qra = jnp.transpose(q_rope * MLA_SCALE, (0, 2, 1, 3)).astype(BF16)
        ka = jnp.transpose(k_nope, (0, 2, 1, 3)).astype(BF16)
        va = jnp.transpose(kv[..., MLA_NOPE:], (0, 2, 1, 3)).astype(BF16)
        kra = kr[:, None].astype(BF16)
        oa = flash_attention(qa, ka, va, lambda h: h, lambda h: h, cfg['t_attn'], qr=qra, kr=kra)
        y_a = jnp.transpose(oa, (0, 2, 1, 3)).reshape(B, Ts, MLA_HEADS * MLA_DV)
    else:
        ck, cv, cc, cr, page_table = past
        q1 = jnp.transpose(dq[:, :, :, 0] * DIFF_SCALE, (0, 2, 1, 3)).reshape(B, DIFF_HEADS * Ts, DIFF_D)
        q2 = jnp.transpose(dq[:, :, :, 1] * DIFF_SCALE, (0, 2, 1, 3)).reshape(B, DIFF_HEADS * Ts, DIFF_D)
        z = jnp.zeros_like(q1)
        qd = jnp.concatenate([jnp.concatenate([q1, z], axis=-1), jnp.concatenate([z, q2], axis=-1)],
                             axis=1).astype(BF16)
        qn = jnp.transpose(q_nope * (w['g_akn'] * MLA_SCALE), (0, 2, 1, 3))
        qn = jnp.einsum('bhtd,hg->bhtgd', qn, jnp.eye(MLA_HEADS, dtype=F32)).reshape(
            B, MLA_HEADS * Ts, MLA_HEADS * MLA_NOPE).astype(BF16)
        qrs = jnp.transpose(q_rope * MLA_SCALE, (0, 2, 1, 3)).reshape(B, MLA_HEADS * Ts, MLA_ROPE).astype(BF16)
        od, oa = sample_attention(l, page_table, lam.reshape(1, 1), qd, k_new[:, :, 0], v_new[:, :, 0],
                                  qn, qrs, wl['wkt'], wl['wv'], ckv, jnp.swapaxes(kr, 1, 2), ck, cv, cc, cr)
        o_diff = jnp.transpose(od.reshape(B, DIFF_HEADS, Ts, 2 * DIFF_D), (0, 2, 1, 3))
        y_a = jnp.transpose(oa.reshape(B, MLA_HEADS, Ts, MLA_DV), (0, 2, 1, 3)).reshape(B, Ts, MLA_HEADS * MLA_DV)

    y_d = (_gnorm(o_diff, w['g_dsub']) * (1.0 - lam_init)).reshape(B, Ts, DIFF_HEADS * 2 * DIFF_D)

    ig = (col('mi') + w['b_mi']).reshape(B, Ts, MLSTM_HEADS)
    fg = (col('mf') + w['b_mf']).reshape(B, Ts, MLSTM_HEADS)
    ml, mc0, mn0, mm0 = mstate
    y_m, mc, mn, mm = mlstm(ml, proj.reshape(B, Ts, -1), lay, ig, fg, w['g_mout'], mc0, mn0, mm0,
                            math.gcd(Ts, cfg['mlstm_chunk']))

    mix = jnp.concatenate([y_d.astype(BF16), y_m, y_a.astype(BF16)], axis=-1).reshape(T, -1)
    x, h2, logits = out_proj(l, mix, x, g1, wl['w_out'], w['g_norm2'], sc2, sh2, wl['w_router'], w['b_router'],
                             cfg['tm_out'], rows_per_seq)
    state = (k_new, v_new, ckv, kr, mc, mn, mm)
    return x, h2, logits, state


def kernel(x_prompt, x_sample, c_prompt, c_sample, cache_diff_k, cache_diff_v, cache_mla_ckv, cache_mla_kr, state_mlstm_c, state_mlstm_n, state_mlstm_m, page_table, w_ada, b_ada, g_norm1, g_norm2, w_in, g_dq, g_dk, lam_diff, g_dsub, b_mi, b_mf, g_mout, g_aqa, w_aqb, g_akva, w_akvb, g_aqn, g_aqr, g_akn, g_akr, w_out, w_router, b_router, w_gu, b_gu, w_down, b_down):
    Bp, Tp, D = x_prompt.shape
    Bs, Ts, _ = x_sample.shape
    depth = w_in.shape[0]
    n_in_pad = -(-w_in.shape[2] // (5 * LANES)) * (5 * LANES)
    cfg = dict(layout=_in_layout(), tm_in=512, tn_in=n_in_pad // 5, tm_mm=512, tm_out=256,
               t_attn=512, mlstm_chunk=256, tm_moe=256)

    n_c = Bp + Bs
    c_all = jnp.concatenate([c_prompt, c_sample, jnp.zeros((-n_c % SUBLANES, D), F32)], axis=0)
    mod = ada_mod(c_all, w_ada, b_ada)[:, :n_c].reshape(depth, n_c, N_MOD, D)
    kvb = w_akvb.reshape(depth, MLA_KV_RANK, MLA_HEADS, MLA_NOPE + MLA_DV)
    wl = dict(
        w_in=_reorder_w_in(w_in, n_in_pad), w_out=w_out.astype(BF16), w_aqb=w_aqb.astype(BF16),
        w_akvb=w_akvb.astype(BF16), w_router=w_router,
        wkt=jnp.transpose(kvb[..., :MLA_NOPE], (0, 2, 3, 1)).reshape(
            depth, MLA_HEADS * MLA_NOPE, MLA_KV_RANK).astype(BF16),
        wv=kvb[..., MLA_NOPE:].reshape(depth, MLA_KV_RANK, MLA_HEADS * MLA_DV).astype(BF16))

    n_pool = cache_diff_k.shape[1]
    page = cache_diff_k.shape[2]
    ck = cache_diff_k.reshape(depth, n_pool, page, 2 * DIFF_D)
    cv = cache_diff_v.reshape(depth, n_pool, page, 2 * DIFF_D)
    cr = jnp.swapaxes(cache_mla_kr, 2, 3)
    past = (ck, cv, cache_mla_ckv, cr, page_table)

    pos_p = jnp.arange(Tp, dtype=jnp.int32)
    pos_s = PAST_LEN + jnp.arange(Ts, dtype=jnp.int32)
    zero_state = (0, jnp.zeros((1, Bp, MLSTM_HEADS, MLSTM_DV, MLSTM_DQK), F32),
                  jnp.zeros((1, Bp, MLSTM_HEADS, MLSTM_DQK), F32),
                  jnp.zeros((1, Bp, MLSTM_HEADS), F32))

    xp = x_prompt.reshape(Bp * Tp, D)
    xs = x_sample.reshape(Bs * Ts, D)
    new_p = [[] for _ in range(7)]
    new_s = [[] for _ in range(7)]
    for l in range(depth):
        w = dict(g_norm1=g_norm1[l], g_norm2=g_norm2[l], g_dq=g_dq[l], g_dk=g_dk[l],
                 g_dsub=g_dsub[l], b_mi=b_mi[l], b_mf=b_mf[l], g_mout=g_mout[l], g_aqa=g_aqa[l],
                 g_akva=g_akva[l], g_aqn=g_aqn[l], g_aqr=g_aqr[l], g_akn=g_akn[l], g_akr=g_akr[l],
                 b_router=b_router[l])
        lam_init = 0.8 - 0.6 * math.exp(-0.3 * l)
        lv = lam_diff[l]
        lam = jnp.exp(jnp.sum(lv[0] * lv[1])) - jnp.exp(jnp.sum(lv[2] * lv[3])) + lam_init
        mods_p = tuple(mod[l, :Bp, i][:, None, :] for i in range(N_MOD))
        mods_s = tuple(jnp.repeat(mod[l, Bp:, i], Ts, axis=0) for i in range(N_MOD))

        xp, h2p, lgp, st_p = _layer(l, xp, mods_p, Tp, (Bp, Tp), pos_p, w, wl, lam, lam_init, None, zero_state, cfg)
        xs, h2s, lgs, st_s = _layer(l, xs, mods_s, Ts, (Bs, Ts), pos_s, w, wl, lam, lam_init, past,
                                    (l, state_mlstm_c, state_mlstm_n, state_mlstm_m), cfg)

        y = moe(l, jnp.concatenate([h2p, h2s], axis=0), jnp.concatenate([lgp, lgs], axis=0),
                w_gu, b_gu, w_down, b_down, cfg['tm_moe'])
        xp = (xp.reshape(Bp, Tp, D) + mods_p[5] * y[:Bp * Tp].reshape(Bp, Tp, D)).reshape(Bp * Tp, D)
        xs = xs + mods_s[5] * y[Bp * Tp:]
        for lst, a in zip(new_p, st_p):
            lst.append(a)
        for lst, a in zip(new_s, st_s):
            lst.append(a)

    outs_p = [jnp.stack(a) for a in new_p]
    outs_s = [jnp.stack(a) for a in new_s]
    return (xp.reshape(Bp, Tp, D), xs.reshape(Bs, Ts, D), *outs_p, *outs_s)
```
